```python
import math
import jax, jax.numpy as jnp
from jax import lax
import numpy as np

D_MODEL = 1024
BATCH = 16
SEQ = 256
DEPTH = 2
DEC_BATCH = 8
DEC_SEQ = 2048
PAST_LEN = 256

GRID_W = 64
N_EVEN = (DEPTH + 1) // 2
N_ODD = DEPTH // 2
N_Q_HEADS = 8
N_KV_HEADS = 2
GROUP = N_Q_HEADS // N_KV_HEADS
HEAD_DIM = 64
ATTN_W = N_Q_HEADS * HEAD_DIM
KV_W = N_KV_HEADS * HEAD_DIM
WINDOW = 128
BLOCK = 128
ATTN_SCALE = HEAD_DIM ** -0.5
ROPE_BASE = 10000.0
POOL_WINDOWS = (2, 4, 8, 16)
N_POOL_GROUPS = 4
POOL_W = D_MODEL // 2
POOL_GW = POOL_W // N_POOL_GROUPS
EVEN_IN = ATTN_W + 2 * KV_W + POOL_W
EVEN_CAT = ATTN_W + POOL_W
LRU_W = D_MODEL
LRU_HEADS = 8
LRU_HD = LRU_W // LRU_HEADS
CONV_W = 4
LRU_C = 8.0
ODD_IN = 2 * LRU_W
N_KEYS = 128
N_EXPERTS = N_KEYS * N_KEYS
PEER_HEADS = 8
PEER_QD = 256
PEER_TOPK = 16
PEER_BLOCK = 128
ALPHA = (2 * DEPTH) ** 0.25
BETA = (8 * DEPTH) ** -0.25
LN_EPS = 1e-5

kernel_name = 'hybrid_diffusion_ctx_prefix_step'


def layer_norm(x, g, b):
    xf = x.astype(jnp.float32)
    mu = xf.mean(-1, keepdims=True)
    var = jnp.square(xf - mu).mean(-1, keepdims=True)
    return ((xf - mu) * lax.rsqrt(var + LN_EPS)).astype(x.dtype) * g + b


def axial_rope(n_tokens):
    rows = n_tokens // GRID_W
    row = jnp.repeat(jnp.arange(rows, dtype=jnp.float32), GRID_W)
    col = (jnp.arange(rows * GRID_W) % GRID_W).astype(jnp.float32)
    n_freq = HEAD_DIM // 4
    inv = ROPE_BASE ** (-jnp.arange(n_freq, dtype=jnp.float32) / n_freq)
    ang = jnp.concatenate([row[:, None] * inv, col[:, None] * inv], axis=-1)
    return jnp.cos(ang), jnp.sin(ang)


def apply_rope(x, cos, sin):
    half = HEAD_DIM // 2
    c = cos[None, :, None, :].astype(x.dtype)
    s = sin[None, :, None, :].astype(x.dtype)
    x1, x2 = x[..., :half], x[..., half:]
    return jnp.concatenate([x1 * c - x2 * s, x2 * c + x1 * s], axis=-1)


def context_attention(q, k, v, sink):
    B, S = q.shape[:2]
    nb = S // BLOCK
    qb = q.reshape(B, nb, BLOCK, N_KV_HEADS, GROUP, HEAD_DIM).transpose(1, 0, 2, 3, 4, 5)
    sk = sink.astype(jnp.float32).reshape(N_KV_HEADS, GROUP)[None, :, :, None]

    def one_block(qblk):
        s = jnp.einsum('bqkgd,bskd->bkgqs', qblk, k).astype(jnp.float32) * ATTN_SCALE
        m = jnp.maximum(s.max(-1), sk)
        p = jnp.exp(s - m[..., None])
        den = p.sum(-1) + jnp.exp(sk - m)
        return jnp.einsum('bkgqs,bskd->bqkgd', (p / den[..., None]).astype(v.dtype), v)

    o = lax.map(one_block, qb)
    return o.transpose(1, 0, 2, 3, 4, 5).reshape(B, S, ATTN_W)


def latent_attention(q, k, v, ck, cv, sink):
    B, S = q.shape[:2]
    nb = S // BLOCK
    qb = q.reshape(B, nb, BLOCK, N_KV_HEADS, GROUP, HEAD_DIM)
    pad = ((0, 0), (BLOCK, BLOCK), (0, 0), (0, 0))
    kp = jnp.pad(k, pad).reshape(B, nb + 2, BLOCK, N_KV_HEADS, HEAD_DIM)
    vp = jnp.pad(v, pad).reshape(B, nb + 2, BLOCK, N_KV_HEADS, HEAD_DIM)
    kw = jnp.concatenate([kp[:, :-2], kp[:, 1:-1], kp[:, 2:]], axis=2)
    vw = jnp.concatenate([vp[:, :-2], vp[:, 1:-1], vp[:, 2:]], axis=2)
    qi = jnp.arange(BLOCK)[:, None]
    kj = jnp.arange(3 * BLOCK)[None, :]
    rel = kj - BLOCK - qi
    jpos = (jnp.arange(nb)[:, None, None] - 1) * BLOCK + kj[None]
    mask = (jnp.abs(rel) <= WINDOW)[None] & (jpos >= 0) & (jpos < S)
    s_lat = jnp.einsum('bnqkgd,bnskd->bnkgqs', qb, kw).astype(jnp.float32) * ATTN_SCALE
    s_lat = jnp.where(mask[None, :, None, None], s_lat, -jnp.inf)
    s_ctx = jnp.einsum('bnqkgd,blkd->bnkgql', qb, ck).astype(jnp.float32) * ATTN_SCALE
    sk = sink.astype(jnp.float32).reshape(N_KV_HEADS, GROUP)[None, None, :, :, None]
    m = jnp.maximum(jnp.maximum(s_lat.max(-1), s_ctx.max(-1)), sk)
    p_lat = jnp.exp(s_lat - m[..., None])
    p_ctx = jnp.exp(s_ctx - m[..., None])
    den = (p_lat.sum(-1) + p_ctx.sum(-1) + jnp.exp(sk - m))[..., None]
    o = (jnp.einsum('bnkgqs,bnskd->bnqkgd', (p_lat / den).astype(v.dtype), vw)
         + jnp.einsum('bnkgql,blkd->bnqkgd', (p_ctx / den).astype(cv.dtype), cv))
    return o.reshape(B, S, ATTN_W)


def multiscale_pool(p, pool_w, pool_scale):
    B, S, _ = p.shape
    pg = p.reshape(B, S, N_POOL_GROUPS, POOL_GW)
    cs = jnp.concatenate([jnp.zeros((B, 1, N_POOL_GROUPS, POOL_GW), jnp.float32),
                          jnp.cumsum(pg.astype(jnp.float32), axis=1)], axis=1)
    t = jnp.arange(S)
    outs = []
    for g, w in enumerate(POOL_WINDOWS):
        lo = w // 2
        hi = w - lo - 1
        start = jnp.clip(t - lo, 0, S)
        end = jnp.clip(t + hi + 1, 0, S)
        mean = (cs[:, end, g] - cs[:, start, g]) / (end - start).astype(jnp.float32)[None, :, None]
        outs.append(mean.astype(p.dtype) - pg[:, :, g])
    pooled = jnp.stack(outs, axis=2)
    y = jnp.einsum('bsgc,gcd->bsgd', pooled, pool_w).reshape(B, S, POOL_W)
    return y * pool_scale


def even_mixer(u, ctx_kv, w_in, sink, pool_w, pool_scale, w_out):
    B, S, _ = u.shape
    h = u @ w_in
    q = h[..., :ATTN_W].reshape(B, S, N_Q_HEADS, HEAD_DIM)
    k = h[..., ATTN_W:ATTN_W + KV_W].reshape(B, S, N_KV_HEADS, HEAD_DIM)
    v = h[..., ATTN_W + KV_W:ATTN_W + 2 * KV_W].reshape(B, S, N_KV_HEADS, HEAD_DIM)
    p = h[..., ATTN_W + 2 * KV_W:]
    if ctx_kv is None:
        attn = context_attention(q, k, v, sink)
        new_kv = (k, v)
    else:
        cos, sin = axial_rope(S)
        attn = latent_attention(apply_rope(q, cos, sin), apply_rope(k, cos, sin), v,
                                ctx_kv[0], ctx_kv[1], sink)
        new_kv = None
    pooled = multiscale_pool(p, pool_w, pool_scale)
    return jnp.concatenate([attn, pooled], axis=-1) @ w_out, new_kv


def centred_dwconv(x, w, b):
    S = x.shape[1]
    left = CONV_W // 2
    right = CONV_W - 1 - left
    xp = jnp.pad(x, ((0, 0), (left, right), (0, 0)))
    acc = xp[:, 0:S] * w[0]
    for j in range(1, CONV_W):
        acc = acc + xp[:, j:j + S] * w[j]
    return acc + b


def rglru_coeffs(x, wa, ba, wx, bx, lam):
    B, S, _ = x.shape
    xh = x.reshape(B, S, LRU_HEADS, LRU_HD)
    r = jax.nn.sigmoid((jnp.einsum('bshi,hij->bshj', xh, wa).reshape(B, S, LRU_W) + ba).astype(jnp.float32))
    i = jax.nn.sigmoid((jnp.einsum('bshi,hij->bshj', xh, wx).reshape(B, S, LRU_W) + bx).astype(jnp.float32))
    log_a = -LRU_C * r * jax.nn.softplus(-lam.astype(jnp.float32))
    a = jnp.exp(log_a)
    b = jnp.sqrt(-jnp.expm1(2.0 * log_a)) * (i * x.astype(jnp.float32))
    return a, b


def linear_scan(a, b, h0, reverse):
    if reverse:
        b = b.at[:, -1].add(a[:, -1] * h0)
    else:
        b = b.at[:, 0].add(a[:, 0] * h0)

    def combine(l, r):
        return l[0] * r[0], r[0] * l[1] + r[1]

    _, h = lax.associative_scan(combine, (a, b), reverse=reverse, axis=1)
    return h


def odd_mixer(u, h0, w_in, conv_w, conv_b, wa, ba, wx, bx, lam, w_out):
    B, S, _ = u.shape
    h = u @ w_in
    xr, xg = h[..., :LRU_W], h[..., LRU_W:]
    xc = centred_dwconv(xr, conv_w, conv_b)
    a_f, b_f = rglru_coeffs(xc, wa[0], ba[0], wx[0], bx[0], lam[0])
    a_b, b_b = rglru_coeffs(xc, wa[1], ba[1], wx[1], bx[1], lam[1])
    is_ctx = h0 is None
    if is_ctx:
        h0 = jnp.zeros((B, 2, LRU_W), jnp.float32)
    hf = linear_scan(a_f, b_f, h0[:, 0].astype(jnp.float32), reverse=False)
    hb = linear_scan(a_b, b_b, h0[:, 1].astype(jnp.float32), reverse=True)
    y = ((hf + hb).astype(u.dtype) * jax.nn.gelu(xg)) @ w_out
    final = jnp.stack([hf[:, -1], hb[:, 0]], axis=1).astype(u.dtype) if is_ctx else None
    return y, final


def peer(x, wq, k1, k2, u_tab, v_tab):
    B, S, _ = x.shape
    T = B * S
    xt = x.reshape(T, D_MODEL)
    q = (xt @ wq).reshape(T, PEER_HEADS, 2, PEER_QD // 2)
    s1 = jnp.einsum('thd,nd->thn', q[:, :, 0], k1).astype(jnp.float32)
    s2 = jnp.einsum('thd,nd->thn', q[:, :, 1], k2).astype(jnp.float32)
    v1, i1 = lax.top_k(s1, PEER_TOPK)
    v2, i2 = lax.top_k(s2, PEER_TOPK)
    n_cand = PEER_TOPK * PEER_TOPK
    cand_s = (v1[..., :, None] + v2[..., None, :]).reshape(T, PEER_HEADS, n_cand)
    cand_i = (i1[..., :, None] * N_KEYS + i2[..., None, :]).reshape(T, PEER_HEADS, n_cand)
    top_s, pos = lax.top_k(cand_s, PEER_TOPK)
    idx = jnp.take_along_axis(cand_i, pos, axis=-1)
    g = jax.nn.softmax(top_s, axis=-1).astype(x.dtype)
    nb = T // PEER_BLOCK

    def expert_block(args):
        xb, ib, gb = args
        act = jax.nn.gelu(jnp.einsum('thkd,td->thk', u_tab[ib], xb))
        return jnp.einsum('thk,thkd->td', gb * act, v_tab[ib])

    out = lax.map(expert_block, (xt.reshape(nb, PEER_BLOCK, D_MODEL),
                                 idx.reshape(nb, PEER_BLOCK, PEER_HEADS, PEER_TOPK),
                                 g.reshape(nb, PEER_BLOCK, PEER_HEADS, PEER_TOPK)))
    return out.reshape(B, S, D_MODEL)


def run_trunk(x, cond, cache_k, cache_v, state_lru, W):
    is_ctx = cache_k is None
    new_k, new_v, new_s = [], [], []
    for layer in range(DEPTH):
        mod = (jax.nn.silu(cond) @ W['ada_w'][layer] + W['ada_b'][layer]).reshape(cond.shape[0], 6, D_MODEL)
        shift1, scale1, gate1, shift2, scale2, gate2 = [mod[:, None, j] for j in range(6)]
        u = x * (1 + scale1) + shift1
        if layer % 2 == 0:
            e = layer // 2
            ctx_kv = None if is_ctx else (cache_k[:, e], cache_v[:, e])
            y, kv = even_mixer(u, ctx_kv, W['even_w_in'][e], W['attn_sink'][e], W['pool_w'][e],
                               W['pool_scale'][e], W['even_w_out'][e])
            if is_ctx:
                new_k.append(kv[0])
                new_v.append(kv[1])
        else:
            o = layer // 2
            h0 = None if is_ctx else state_lru[:, o]
            y, st = odd_mixer(u, h0, W['odd_w_in'][o], W['conv_w'][o], W['conv_b'][o],
                              W['gate_a_w'][o], W['gate_a_b'][o], W['gate_x_w'][o], W['gate_x_b'][o],
                              W['lru_lambda'][o], W['odd_w_out'][o])
            if is_ctx:
                new_s.append(st)
        x = layer_norm(ALPHA * x + gate1 * y, W['ln1_g'][layer], W['ln1_b'][layer])
        u = x * (1 + scale2) + shift2
        y = peer(u, W['peer_wq'][layer], W['peer_k1'][layer], W['peer_k2'][layer],
                 W['peer_u'][layer], W['peer_v'][layer])
        x = layer_norm(ALPHA * x + gate2 * y, W['ln2_g'][layer], W['ln2_b'][layer])
    if is_ctx:
        return x, jnp.stack(new_k, axis=1), jnp.stack(new_v, axis=1), jnp.stack(new_s, axis=1)
    return x


def setup_inputs(seed: int = 0) -> dict:
    key = jax.random.key(seed)
    ks = iter(jax.random.split(key, 48))
    f32 = jnp.float32

    def nrm(shape, s):
        return jax.random.normal(next(ks), shape, f32) * s

    D = D_MODEL
    x_prompt = nrm((BATCH, SEQ, D), 1.0)
    x_sample = nrm((DEC_BATCH, DEC_SEQ, D), 1.0)
    cache_attn_k = nrm((DEC_BATCH, N_EVEN, PAST_LEN, N_KV_HEADS, HEAD_DIM), 1.0)
    cache_attn_v = nrm((DEC_BATCH, N_EVEN, PAST_LEN, N_KV_HEADS, HEAD_DIM), 1.0)
    state_lru = nrm((DEC_BATCH, N_ODD, 2, LRU_W), 0.5)
    c = nrm((DEC_BATCH, D), 1.0)
    c_ctx = nrm((D,), 1.0)
    ada_w = nrm((DEPTH, D, 6 * D), D ** -0.5)
    ada_b = nrm((DEPTH, 6 * D), 0.02)
    ln1_g = 1.0 + nrm((DEPTH, D), 0.05)
    ln1_b = nrm((DEPTH, D), 0.02)
    ln2_g = 1.0 + nrm((DEPTH, D), 0.05)
    ln2_b = nrm((DEPTH, D), 0.02)
    even_w_in = nrm((N_EVEN, D, EVEN_IN), D ** -0.5)
    attn_sink = nrm((N_EVEN, N_Q_HEADS), 0.5)
    pool_w = nrm((N_EVEN, N_POOL_GROUPS, POOL_GW, POOL_GW), POOL_GW ** -0.5)
    pool_scale = 1.0 + nrm((N_EVEN, POOL_W), 0.1)
    even_w_out = nrm((N_EVEN, EVEN_CAT, D), BETA * EVEN_CAT ** -0.5)
    odd_w_in = nrm((N_ODD, D, ODD_IN), D ** -0.5)
    conv_w = nrm((N_ODD, CONV_W, LRU_W), CONV_W ** -0.5)
    conv_b = nrm((N_ODD, LRU_W), 0.02)
    gate_a_w = nrm((N_ODD, 2, LRU_HEADS, LRU_HD, LRU_HD), LRU_HD ** -0.5)
    gate_a_b = nrm((N_ODD, 2, LRU_W), 0.02)
    gate_x_w = nrm((N_ODD, 2, LRU_HEADS, LRU_HD, LRU_HD), LRU_HD ** -0.5)
    gate_x_b = nrm((N_ODD, 2, LRU_W), 0.02)
    a8 = jax.random.uniform(next(ks), (N_ODD, 2, LRU_W), f32, 0.9, 0.999)
    s = a8 ** (1.0 / LRU_C)
    lru_lambda = jnp.log(s) - jnp.log1p(-s)
    odd_w_out = nrm((N_ODD, LRU_W, D), BETA * LRU_W ** -0.5)
    peer_wq = nrm((DEPTH, D, PEER_HEADS * PEER_QD), D ** -0.5)
    peer_k1 = nrm((DEPTH, N_KEYS, PEER_QD // 2), (PEER_QD // 2) ** -0.5)
    peer_k2 = nrm((DEPTH, N_KEYS, PEER_QD // 2), (PEER_QD // 2) ** -0.5)
    peer_u = nrm((DEPTH, N_EXPERTS, D), D ** -0.5)
    peer_v = nrm((DEPTH, N_EXPERTS, D), BETA * PEER_HEADS ** -0.5)
    return {'x_prompt': x_prompt, 'x_sample': x_sample, 'cache_attn_k': cache_attn_k,
            'cache_attn_v': cache_attn_v, 'state_lru': state_lru, 'c': c, 'c_ctx': c_ctx,
            'ada_w': ada_w, 'ada_b': ada_b, 'ln1_g': ln1_g, 'ln1_b': ln1_b, 'ln2_g': ln2_g,
            'ln2_b': ln2_b, 'even_w_in': even_w_in, 'attn_sink': attn_sink, 'pool_w': pool_w,
            'pool_scale': pool_scale, 'even_w_out': even_w_out, 'odd_w_in': odd_w_in,
            'conv_w': conv_w, 'conv_b': conv_b, 'gate_a_w': gate_a_w, 'gate_a_b': gate_a_b,
            'gate_x_w': gate_x_w, 'gate_x_b': gate_x_b, 'lru_lambda': lru_lambda,
            'odd_w_out': odd_w_out, 'peer_wq': peer_wq, 'peer_k1': peer_k1, 'peer_k2': peer_k2,
            'peer_u': peer_u, 'peer_v': peer_v}


def reference(x_prompt, x_sample, cache_attn_k, cache_attn_v, state_lru, c, c_ctx,
              ada_w, ada_b, ln1_g, ln1_b, ln2_g, ln2_b, even_w_in, attn_sink, pool_w,
              pool_scale, even_w_out, odd_w_in, conv_w, conv_b, gate_a_w, gate_a_b,
              gate_x_w, gate_x_b, lru_lambda, odd_w_out, peer_wq, peer_k1, peer_k2,
              peer_u, peer_v):
    W = dict(ada_w=ada_w, ada_b=ada_b, ln1_g=ln1_g, ln1_b=ln1_b, ln2_g=ln2_g, ln2_b=ln2_b,
             even_w_in=even_w_in, attn_sink=attn_sink, pool_w=pool_w, pool_scale=pool_scale,
             even_w_out=even_w_out, odd_w_in=odd_w_in, conv_w=conv_w, conv_b=conv_b,
             gate_a_w=gate_a_w, gate_a_b=gate_a_b, gate_x_w=gate_x_w, gate_x_b=gate_x_b,
             lru_lambda=lru_lambda, odd_w_out=odd_w_out, peer_wq=peer_wq, peer_k1=peer_k1,
             peer_k2=peer_k2, peer_u=peer_u, peer_v=peer_v)
    y_prompt, new_attn_k, new_attn_v, new_state_lru = run_trunk(
        x_prompt, c_ctx[None, :], None, None, None, W)
    y_sample = run_trunk(x_sample, c, cache_attn_k, cache_attn_v, state_lru, W)
    return (y_prompt, y_sample, new_attn_k, new_attn_v, new_state_lru)
```

```python
import functools
import math

import jax
import jax.numpy as jnp
from jax import lax
from jax.experimental import pallas as pl
from jax.experimental.pallas import tpu as pltpu

F32 = jnp.float32
BF16 = jnp.bfloat16

D_MODEL = 1024
DEPTH = 2
GRID_W = 64
N_Q_HEADS = 8
N_KV_HEADS = 2
GROUP = N_Q_HEADS // N_KV_HEADS
HEAD_DIM = 64
ATTN_W = N_Q_HEADS * HEAD_DIM
KV_W = N_KV_HEADS * HEAD_DIM
ATTN_BLOCK = 128
ATTN_SCALE = HEAD_DIM ** -0.5
ROPE_BASE = 10000.0
POOL_WINDOWS = (2, 4, 8, 16)
POOL_W = D_MODEL // 2
POOL_GW = POOL_W // len(POOL_WINDOWS)
POOL_HALO = 8
EVEN_IN = ATTN_W + 2 * KV_W + POOL_W
LRU_W = D_MODEL
LRU_HEADS = 8
LRU_HD = LRU_W // LRU_HEADS
CONV_W = 4
CONV_HALO = 8
LRU_C = 8.0
N_KEYS = 128
N_EXPERTS = N_KEYS * N_KEYS
PEER_HEADS = 8
PEER_QD = 256
PEER_TOPK = 16
ALPHA = (2 * DEPTH) ** 0.25
LN_EPS = 1e-5
NEG_INF = float("-inf")

LANES = 128
SUBLANES = 8
VMEM_LIMIT_BYTES = 56 * 1024 * 1024

_CAND_COLS = tuple(min(PEER_TOPK, PEER_TOPK // (i + 1)) for i in range(PEER_TOPK))


def _cparams(sem):
    return pltpu.CompilerParams(dimension_semantics=sem, vmem_limit_bytes=VMEM_LIMIT_BYTES)


def _modulate(x, shift, scale):
    return x * (1.0 + scale) + shift


def _layer_norm(z, g, b):
    mu = jnp.mean(z, axis=-1, keepdims=True)
    zc = z - mu
    var = jnp.mean(zc * zc, axis=-1, keepdims=True)
    return zc * lax.rsqrt(var + LN_EPS) * g + b


def _ada_body(c_ref, w_ref, b_ref, o_ref):
    c = c_ref[...]
    s = (c * jax.nn.sigmoid(c)).astype(BF16)
    o_ref[0] = jnp.dot(s, w_ref[0].astype(BF16), preferred_element_type=F32) + b_ref[0]


def _ada(cond, ada_w, ada_b):
    rows = cond.shape[0]
    n_out = 6 * D_MODEL
    nb = 1536
    out = pl.pallas_call(
        _ada_body,
        grid=(DEPTH, n_out // nb),
        in_specs=[
            pl.BlockSpec((rows, D_MODEL), lambda l, n: (0, 0)),
            pl.BlockSpec((1, D_MODEL, nb), lambda l, n: (l, 0, n)),
            pl.BlockSpec((1, 1, nb), lambda l, n: (l, 0, n)),
        ],
        out_specs=pl.BlockSpec((1, rows, nb), lambda l, n: (l, 0, n)),
        out_shape=jax.ShapeDtypeStruct((DEPTH, rows, n_out), F32),
        compiler_params=_cparams(("arbitrary", "arbitrary")),
        name="ada_mod",
    )(cond, ada_w, ada_b.reshape(DEPTH, 1, n_out))
    return out.reshape(DEPTH, rows, 6, D_MODEL)


def _rope_tables(seq):
    rows = seq // GRID_W
    row = jnp.repeat(jnp.arange(rows, dtype=F32), GRID_W)
    col = (jnp.arange(rows * GRID_W) % GRID_W).astype(F32)
    n_freq = HEAD_DIM // 4
    inv = ROPE_BASE ** (-jnp.arange(n_freq, dtype=F32) / n_freq)
    ang = jnp.concatenate([row[:, None] * inv, col[:, None] * inv], axis=-1)
    c, s = jnp.cos(ang), jnp.sin(ang)
    cos = jnp.concatenate([c, c, c, c], axis=-1)
    sin = jnp.concatenate([-s, s, -s, s], axis=-1)
    return cos, sin


def _rope128(x, cos, sin, lane_lo):
    rot = jnp.where(lane_lo, pltpu.roll(x, LANES - HEAD_DIM // 2, 1), pltpu.roll(x, HEAD_DIM // 2, 1))
    return x * cos + rot * sin


def _even_in_body(*refs, rope):
    if rope:
        x_ref, mod_ref, w_ref, cos_ref, sin_ref, q_ref, k_ref, v_ref, p_ref = refs
    else:
        x_ref, mod_ref, w_ref, q_ref, k_ref, v_ref, p_ref = refs
    u = _modulate(x_ref[0], mod_ref[0, 0:1, :], mod_ref[0, 1:2, :]).astype(BF16)
    h = jnp.dot(u, w_ref[...], preferred_element_type=F32)
    q = h[:, :ATTN_W]
    k = h[:, ATTN_W:ATTN_W + KV_W]
    if rope:
        cos = cos_ref[...]
        sin = sin_ref[...]
        lane = lax.broadcasted_iota(jnp.int32, cos.shape, 1)
        lane_lo = (lane % HEAD_DIM) < (HEAD_DIM // 2)
        q = jnp.concatenate(
            [_rope128(q[:, c * LANES:(c + 1) * LANES], cos, sin, lane_lo) for c in range(ATTN_W // LANES)], axis=1)
        k = _rope128(k, cos, sin, lane_lo)
    q_ref[0] = q.astype(BF16)
    k_ref[0] = k
    v_ref[0] = h[:, ATTN_W + KV_W:ATTN_W + 2 * KV_W]
    p_ref[0] = h[:, ATTN_W + 2 * KV_W:]


def _even_in(x, mod, w_in, rope, tb=256):
    bsz, seq, _ = x.shape
    mod_b = (lambda b: b) if mod.shape[0] > 1 else (lambda b: 0)
    in_specs = [
        pl.BlockSpec((1, tb, D_MODEL), lambda b, i: (b, i, 0)),
        pl.BlockSpec((1, 6, D_MODEL), lambda b, i: (mod_b(b), 0, 0)),
        pl.BlockSpec((D_MODEL, EVEN_IN), lambda b, i: (0, 0)),
    ]
    args = [x, mod, w_in]
    if rope:
        cos, sin = _rope_tables(seq)
        in_specs += [pl.BlockSpec((tb, LANES), lambda b, i: (i, 0))] * 2
        args += [cos, sin]
    return pl.pallas_call(
        functools.partial(_even_in_body, rope=rope),
        grid=(bsz, seq // tb),
        in_specs=in_specs,
        out_specs=[
            pl.BlockSpec((1, tb, ATTN_W), lambda b, i: (b, i, 0)),
            pl.BlockSpec((1, tb, KV_W), lambda b, i: (b, i, 0)),
            pl.BlockSpec((1, tb, KV_W), lambda b, i: (b, i, 0)),
            pl.BlockSpec((1, tb, POOL_W), lambda b, i: (b, i, 0)),
        ],
        out_shape=[
            jax.ShapeDtypeStruct((bsz, seq, ATTN_W), BF16),
            jax.ShapeDtypeStruct((bsz, seq, KV_W), F32),
            jax.ShapeDtypeStruct((bsz, seq, KV_W), F32),
            jax.ShapeDtypeStruct((bsz, seq, POOL_W), F32),
        ],
        compiler_params=_cparams(("arbitrary", "arbitrary")),
        name="even_in_rope" if rope else "even_in",
    )(*args)


def _attend(q, kcat, vcat, sink_ref, bias, o_ref):
    for kv in range(N_KV_HEADS):
        kk = kcat[:, kv * HEAD_DIM:(kv + 1) * HEAD_DIM]
        vv = vcat[:, kv * HEAD_DIM:(kv + 1) * HEAD_DIM]
        for g in range(GROUP):
            hq = kv * GROUP + g
            qh = q[:, hq * HEAD_DIM:(hq + 1) * HEAD_DIM]
            s = lax.dot_general(qh, kk, (((1,), (1,)), ((), ())), preferred_element_type=F32) * ATTN_SCALE
            if bias is not None:
                s = s + bias
            sk = sink_ref[hq]
            m = jnp.maximum(jnp.max(s, axis=-1, keepdims=True), sk)
            p = jnp.exp(s - m)
            den = jnp.sum(p, axis=-1, keepdims=True) + jnp.exp(sk - m)
            o = jnp.dot((p / den).astype(BF16), vv, preferred_element_type=F32)
            o_ref[0, :, hq * HEAD_DIM:(hq + 1) * HEAD_DIM] = o


def _attn_ctx_body(sink_ref, q_ref, k_ref, v_ref, o_ref):
    _attend(q_ref[0], k_ref[0].astype(BF16), v_ref[0].astype(BF16), sink_ref, None, o_ref)


def _attn_ctx(q, k, v, sink):
    bsz, seq, _ = q.shape
    blk = ATTN_BLOCK
    return pl.pallas_call(
        _attn_ctx_body,
        grid=(bsz, seq // blk),
        in_specs=[
            pl.BlockSpec(memory_space=pltpu.SMEM),
            pl.BlockSpec((1, blk, ATTN_W), lambda b, n: (b, n, 0)),
            pl.BlockSpec((1, seq, KV_W), lambda b, n: (b, 0, 0)),
            pl.BlockSpec((1, seq, KV_W), lambda b, n: (b, 0, 0)),
        ],
        out_specs=pl.BlockSpec((1, blk, ATTN_W), lambda b, n: (b, n, 0)),
        out_shape=jax.ShapeDtypeStruct((bsz, seq, ATTN_W), F32),
        compiler_params=_cparams(("arbitrary", "arbitrary")),
        name="attn_ctx",
    )(sink, q, k, v)


def _attn_lat_body(sink_ref, q_ref, kp_ref, kc_ref, kn_ref, vp_ref, vc_ref, vn_ref, ck_ref, cv_ref, o_ref):
    n = pl.program_id(1)
    nb = pl.num_programs(1)
    blk = ATTN_BLOCK
    qi = lax.broadcasted_iota(jnp.int32, (blk, blk), 0)
    kj = lax.broadcasted_iota(jnp.int32, (blk, blk), 1)
    prev_ok = jnp.where(n > 0, 0.0, NEG_INF)
    next_ok = jnp.where(n < nb - 1, 0.0, NEG_INF)
    bias_prev = jnp.where(kj >= qi, 0.0, NEG_INF) + prev_ok
    bias_next = jnp.where(kj <= qi, 0.0, NEG_INF) + next_ok
    n_ctx = ck_ref.shape[1]
    bias = jnp.concatenate(
        [bias_prev, jnp.zeros((blk, blk), F32), bias_next, jnp.zeros((blk, n_ctx), F32)], axis=1)
    kcat = jnp.concatenate([kp_ref[0], kc_ref[0], kn_ref[0], ck_ref[0]], axis=0).astype(BF16)
    vcat = jnp.concatenate([vp_ref[0], vc_ref[0], vn_ref[0], cv_ref[0]], axis=0).astype(BF16)
    _attend(q_ref[0], kcat, vcat, sink_ref, bias, o_ref)


def _attn_lat(q, k, v, ck, cv, sink):
    bsz, seq, _ = q.shape
    blk = ATTN_BLOCK
    nb = seq // blk
    n_ctx = ck.shape[1]
    prev = lambda b, n: (b, jnp.maximum(n - 1, 0), 0)
    cur = lambda b, n: (b, n, 0)
    nxt = lambda b, n: (b, jnp.minimum(n + 1, nb - 1), 0)
    kv_spec = lambda im: pl.BlockSpec((1, blk, KV_W), im)
    ctx_spec = pl.BlockSpec((1, n_ctx, KV_W), lambda b, n: (b, 0, 0))
    return pl.pallas_call(
        _attn_lat_body,
        grid=(bsz, nb),
        in_specs=[
            pl.BlockSpec(memory_space=pltpu.SMEM),
            pl.BlockSpec((1, blk, ATTN_W), cur),
            kv_spec(prev), kv_spec(cur), kv_spec(nxt),
            kv_spec(prev), kv_spec(cur), kv_spec(nxt),
            ctx_spec, ctx_spec,
        ],
        out_specs=pl.BlockSpec((1, blk, ATTN_W), cur),
        out_shape=jax.ShapeDtypeStruct((bsz, seq, ATTN_W), F32),
        compiler_params=_cparams(("arbitrary", "arbitrary")),
        name="attn_lat",
    )(sink, q, k, k, k, v, v, v, ck, cv)


def _pool_body(p_ref, w_ref, sc_ref, o_ref, pad_ref):
    seq = p_ref.shape[1]
    halo = POOL_HALO
    t = lax.broadcasted_iota(jnp.int32, (seq, POOL_GW), 0)
    pad_ref[0:halo, :] = jnp.zeros((halo, POOL_GW), F32)
    pad_ref[halo + seq:2 * halo + seq, :] = jnp.zeros((halo, POOL_GW), F32)
    for g, w in enumerate(POOL_WINDOWS):
        lo = w // 2
        hi = w - lo - 1
        cols = slice(g * POOL_GW, (g + 1) * POOL_GW)
        pg = p_ref[0, :, cols]
        pad_ref[halo:halo + seq, :] = pg
        acc = pad_ref[halo - lo:halo - lo + seq, :]
        for d in range(-lo + 1, hi + 1):
            acc = acc + pad_ref[halo + d:halo + d + seq, :]
        cnt = (jnp.minimum(t + hi + 1, seq) - jnp.maximum(t - lo, 0)).astype(F32)
        pooled = (acc / cnt - pg).astype(BF16)
        y = jnp.dot(pooled, w_ref[g], preferred_element_type=F32)
        o_ref[0, :, cols] = y * sc_ref[:, cols]


def _pool(p, pool_w, pool_scale):
    bsz, seq, _ = p.shape
    n_groups = len(POOL_WINDOWS)
    return pl.pallas_call(
        _pool_body,
        grid=(bsz,),
        in_specs=[
            pl.BlockSpec((1, seq, POOL_W), lambda b: (b, 0, 0)),
            pl.BlockSpec((n_groups, POOL_GW, POOL_GW), lambda b: (0, 0, 0)),
            pl.BlockSpec((1, POOL_W), lambda b: (0, 0)),
        ],
        out_specs=pl.BlockSpec((1, seq, POOL_W), lambda b: (b, 0, 0)),
        out_shape=jax.ShapeDtypeStruct((bsz, seq, POOL_W), F32),
        scratch_shapes=[pltpu.VMEM((seq + 2 * POOL_HALO, POOL_GW), F32)],
        compiler_params=_cparams(("arbitrary",)),
        name="pool",
    )(p, pool_w, pool_scale.reshape(1, POOL_W))


def _even_out_body(attn_ref, pool_ref, x_ref, mod_ref, w_ref, g_ref, b_ref, o_ref):
    cat = jnp.concatenate([attn_ref[0], pool_ref[0]], axis=1).astype(BF16)
    y = jnp.dot(cat, w_ref[...], preferred_element_type=F32)
    z = ALPHA * x_ref[0] + mod_ref[0, 2:3, :] * y
    o_ref[0] = _layer_norm(z, g_ref[...], b_ref[...])


def _even_out(attn, pooled, x, mod, w_out, ln_g, ln_b, tb=256):
    bsz, seq, _ = x.shape
    mod_b = (lambda b: b) if mod.shape[0] > 1 else (lambda b: 0)
    tok = lambda w: pl.BlockSpec((1, tb, w), lambda b, i: (b, i, 0))
    vec = pl.BlockSpec((1, D_MODEL), lambda b, i: (0, 0))
    return pl.pallas_call(
        _even_out_body,
        grid=(bsz, seq // tb),
        in_specs=[
            tok(ATTN_W), tok(POOL_W), tok(D_MODEL),
            pl.BlockSpec((1, 6, D_MODEL), lambda b, i: (mod_b(b), 0, 0)),
            pl.BlockSpec((ATTN_W + POOL_W, D_MODEL), lambda b, i: (0, 0)),
            vec, vec,
        ],
        out_specs=tok(D_MODEL),
        out_shape=jax.ShapeDtypeStruct((bsz, seq, D_MODEL), F32),
        compiler_params=_cparams(("arbitrary", "arbitrary")),
        name="even_out",
    )(attn, pooled, x, mod, w_out, ln_g.reshape(1, D_MODEL), ln_b.reshape(1, D_MODEL))


def _softplus(z):
    return jnp.maximum(z, 0.0) + jnp.log1p(jnp.exp(-jnp.abs(z)))


def _lru_coeffs(xc, gw_ref, ba, bx, lam, a_scr, b_scr):
    sp = _softplus(-lam)
    for hh in range(LRU_HEADS):
        cols = slice(hh * LRU_HD, (hh + 1) * LRU_HD)
        xh = xc[:, cols]
        pre = jnp.dot(xh.astype(BF16), gw_ref[hh], preferred_element_type=F32)
        r = jax.nn.sigmoid(pre[:, :LRU_HD] + ba[:, cols])
        ig = jax.nn.sigmoid(pre[:, LRU_HD:] + bx[:, cols])
        log_a = -LRU_C * r * sp[:, cols]
        a = jnp.exp(log_a)
        b = jnp.sqrt(jnp.tanh(-log_a) * (a * a + 1.0)) * (ig * xh)
        a_scr[:, cols] = a
        b_scr[:, cols] = b


def _conv_input(x_ref, xp_ref, xn_ref, mod_ref, wr_ref, cw_ref, cb_ref, xr_scr, blk, nblk):
    tb = x_ref.shape[1]
    halo = CONV_HALO
    shift = mod_ref[0, 0:1, :]
    scale = mod_ref[0, 1:2, :]
    um = _modulate(x_ref[0], shift, scale).astype(BF16)
    up = _modulate(xp_ref[0], shift, scale).astype(BF16)
    un = _modulate(xn_ref[0], shift, scale).astype(BF16)
    w = wr_ref[...]
    has_prev = jnp.where(blk > 0, 1.0, 0.0)
    has_next = jnp.where(blk < nblk - 1, 1.0, 0.0)
    xr_scr[0:halo, :] = jnp.dot(up, w, preferred_element_type=F32) * has_prev
    xr_scr[halo:halo + tb, :] = jnp.dot(um, w, preferred_element_type=F32)
    xr_scr[halo + tb:2 * halo + tb, :] = jnp.dot(un, w, preferred_element_type=F32) * has_next
    left = CONV_W // 2
    xc = cb_ref[...] + xr_scr[halo - left:halo - left + tb, :] * cw_ref[0:1, :]
    for j in range(1, CONV_W):
        xc = xc + xr_scr[halo - left + j:halo - left + j + tb, :] * cw_ref[j:j + 1, :]
    return um, xc


def _odd_fwd_body(x_ref, xp_ref, xn_ref, mod_ref, wr_ref, cw_ref, cb_ref, gw_ref, ba_ref, bx_ref, lam_ref,
                  h0_ref, hf_ref, fin_ref, xr_scr, a_scr, b_scr, h_scr):
    i = pl.program_id(1)
    nblk = pl.num_programs(1)
    tb = x_ref.shape[1]
    _, xc = _conv_input(x_ref, xp_ref, xn_ref, mod_ref, wr_ref, cw_ref, cb_ref, xr_scr, i, nblk)
    _lru_coeffs(xc, gw_ref, ba_ref[...], bx_ref[...], lam_ref[...], a_scr, b_scr)

    @pl.when(i == 0)
    def _():
        h_scr[...] = h0_ref[0, 0]

    def tile(t8, h):
        off = pl.multiple_of(t8 * SUBLANES, SUBLANES)
        a8 = a_scr[pl.ds(off, SUBLANES), :]
        b8 = b_scr[pl.ds(off, SUBLANES), :]
        rows = []
        for r in range(SUBLANES):
            h = a8[r:r + 1, :] * h + b8[r:r + 1, :]
            rows.append(h)
        hf_ref[0, pl.ds(off, SUBLANES), :] = jnp.concatenate(rows, axis=0)
        return h

    h = lax.fori_loop(0, tb // SUBLANES, tile, h_scr[...])
    h_scr[...] = h
    fin_ref[0] = h


def _odd_bwd_body(x_ref, xp_ref, xn_ref, mod_ref, wr_ref, wg_ref, cw_ref, cb_ref, gw_ref, ba_ref, bx_ref, lam_ref,
                  h0_ref, hf_ref, wo_ref, g_ref, b_ref, o_ref, fin_ref, xr_scr, a_scr, b_scr, h_scr, hs_scr):
    i = pl.program_id(1)
    nblk = pl.num_programs(1)
    blk = nblk - 1 - i
    tb = x_ref.shape[1]
    um, xc = _conv_input(x_ref, xp_ref, xn_ref, mod_ref, wr_ref, cw_ref, cb_ref, xr_scr, blk, nblk)
    _lru_coeffs(xc, gw_ref, ba_ref[...], bx_ref[...], lam_ref[...], a_scr, b_scr)

    @pl.when(i == 0)
    def _():
        h_scr[...] = h0_ref[0, 0]

    n_tiles = tb // SUBLANES

    def tile(k, h):
        off = pl.multiple_of((n_tiles - 1 - k) * SUBLANES, SUBLANES)
        a8 = a_scr[pl.ds(off, SUBLANES), :]
        b8 = b_scr[pl.ds(off, SUBLANES), :]
        rows = []
        for r in reversed(range(SUBLANES)):
            h = a8[r:r + 1, :] * h + b8[r:r + 1, :]
            rows.append(h)
        hb8 = jnp.concatenate(rows[::-1], axis=0)
        hs_scr[pl.ds(off, SUBLANES), :] = hf_ref[0, pl.ds(off, SUBLANES), :] + hb8
        return h

    h = lax.fori_loop(0, n_tiles, tile, h_scr[...])
    h_scr[...] = h
    fin_ref[0] = h
    xg = jnp.dot(um, wg_ref[...], preferred_element_type=F32)
    gated = (hs_scr[...] * jax.nn.gelu(xg)).astype(BF16)
    y = jnp.dot(gated, wo_ref[...], preferred_element_type=F32)
    z = ALPHA * x_ref[0] + mod_ref[0, 2:3, :] * y
    o_ref[0] = _layer_norm(z, g_ref[...], b_ref[...])


def _odd_layer(x, mod, h0, w_r, w_g, conv_w, conv_b, gw, ba, bx, lam, w_out, ln_g, ln_b, tb=256):
    bsz, seq, _ = x.shape
    nblk = seq // tb
    rows8 = seq // CONV_HALO
    per8 = tb // CONV_HALO
    mod_b = (lambda b: b) if mod.shape[0] > 1 else (lambda b: 0)
    vec = lambda im=None: pl.BlockSpec((1, D_MODEL), lambda b, i: (0, 0))
    full = lambda shape: pl.BlockSpec(shape, lambda b, i: tuple(0 for _ in shape))

    def x_specs(blk_of):
        return [
            pl.BlockSpec((1, tb, D_MODEL), lambda b, i: (b, blk_of(i), 0)),
            pl.BlockSpec((1, CONV_HALO, D_MODEL), lambda b, i: (b, jnp.maximum(blk_of(i) * per8 - 1, 0), 0)),
            pl.BlockSpec((1, CONV_HALO, D_MODEL), lambda b, i: (b, jnp.minimum((blk_of(i) + 1) * per8, rows8 - 1), 0)),
            pl.BlockSpec((1, 6, D_MODEL), lambda b, i: (mod_b(b), 0, 0)),
        ]

    gate_specs = [
        full((LRU_HEADS, LRU_HD, 2 * LRU_HD)),
        vec(), vec(), vec(),
    ]
    scan_scratch = [
        pltpu.VMEM((tb + 2 * CONV_HALO, LRU_W), F32),
        pltpu.VMEM((tb, LRU_W), F32),
        pltpu.VMEM((tb, LRU_W), F32),
        pltpu.VMEM((1, LRU_W), F32),
    ]
    row = lambda a: a.reshape(1, D_MODEL)
    state_spec = lambda d: pl.BlockSpec((1, 1, 1, D_MODEL), lambda b, i: (b, d, 0, 0))
    h0 = h0.reshape(bsz, 2, 1, LRU_W)

    fwd_blk = lambda i: i
    hf, fin_f = pl.pallas_call(
        _odd_fwd_body,
        grid=(bsz, nblk),
        in_specs=x_specs(fwd_blk) + [full((D_MODEL, LRU_W)), full((CONV_W, LRU_W)), vec()] + gate_specs
        + [state_spec(0)],
        out_specs=[
            pl.BlockSpec((1, tb, LRU_W), lambda b, i: (b, i, 0)),
            pl.BlockSpec((1, 1, LRU_W), lambda b, i: (b, 0, 0)),
        ],
        out_shape=[
            jax.ShapeDtypeStruct((bsz, seq, LRU_W), F32),
            jax.ShapeDtypeStruct((bsz, 1, LRU_W), F32),
        ],
        scratch_shapes=scan_scratch,
        compiler_params=_cparams(("arbitrary", "arbitrary")),
        name="lru_fwd",
    )(x, x, x, mod, w_r, conv_w, row(conv_b), gw[0], row(ba[0]), row(bx[0]), row(lam[0]), h0)

    bwd_blk = lambda i: nblk - 1 - i
    x1, fin_b = pl.pallas_call(
        _odd_bwd_body,
        grid=(bsz, nblk),
        in_specs=x_specs(bwd_blk) + [full((D_MODEL, LRU_W)), full((D_MODEL, LRU_W)), full((CONV_W, LRU_W)), vec()]
        + gate_specs + [
            state_spec(1),
            pl.BlockSpec((1, tb, LRU_W), lambda b, i: (b, bwd_blk(i), 0)),
            full((LRU_W, D_MODEL)), vec(), vec(),
        ],
        out_specs=[
            pl.BlockSpec((1, tb, D_MODEL), lambda b, i: (b, bwd_blk(i), 0)),
            pl.BlockSpec((1, 1, LRU_W), lambda b, i: (b, 0, 0)),
        ],
        out_shape=[
            jax.ShapeDtypeStruct((bsz, seq, D_MODEL), F32),
            jax.ShapeDtypeStruct((bsz, 1, LRU_W), F32),
        ],
        scratch_shapes=scan_scratch + [pltpu.VMEM((tb, LRU_W), F32)],
        compiler_params=_cparams(("arbitrary", "arbitrary")),
        name="lru_bwd_out",
    )(x, x, x, mod, w_r, w_g, conv_w, row(conv_b), gw[1], row(ba[1]), row(bx[1]), row(lam[1]), h0, hf,
      w_out, row(ln_g), row(ln_b))
    return x1, jnp.concatenate([fin_f, fin_b], axis=1)


def _extract_max(v, row_iota, sentinel):
    m = jnp.max(v, axis=0, keepdims=True)
    idx = jnp.min(jnp.where(v == m, row_iota, sentinel), axis=0, keepdims=True)
    return m, row_iota == idx


def _top16_rows(v):
    n = v.shape[0]
    key_iota = lax.broadcasted_iota(jnp.int32, v.shape, 0).astype(F32)
    r_iota = lax.broadcasted_iota(jnp.int32, (PEER_TOPK, v.shape[1]), 0)

    def body(r, carry):
        v, rank, vals = carry
        m, sel = _extract_max(v, key_iota, float(n))
        rank = jnp.where(sel, r.astype(F32), rank)
        v = jnp.where(sel, NEG_INF, v)
        vals = jnp.where(r_iota == r, m, vals)
        return v, rank, vals

    init = (v, jnp.full(v.shape, float(N_KEYS - 1), F32), jnp.zeros((PEER_TOPK, v.shape[1]), F32))
    _, rank, vals = lax.fori_loop(0, PEER_TOPK, body, init)
    return rank, vals


def _select_pairs(v1, v2):
    n = v1.shape[1]
    row8 = lax.broadcasted_iota(jnp.int32, (SUBLANES, n), 0)
    pieces = []
    for i in range(SUBLANES):
        cols = _CAND_COLS[i]
        for j0 in range(0, cols, SUBLANES):
            c = v1[i:i + 1, :] + v2[j0:j0 + SUBLANES, :]
            if cols - j0 < SUBLANES:
                c = jnp.where(row8 < cols - j0, c, NEG_INF)
            pieces.append(c)
    pieces.append(v1[SUBLANES:, :] + v2[0:1, :])
    cand = jnp.concatenate(pieces, axis=0)
    n_rows = cand.shape[0]
    c_iota = lax.broadcasted_iota(jnp.int32, cand.shape, 0).astype(F32)
    r_iota = lax.broadcasted_iota(jnp.int32, (PEER_TOPK, n), 0)

    def body(r, carry):
        cand, picked, tops = carry
        m, sel = _extract_max(cand, c_iota, float(n_rows))
        cand = jnp.where(sel, NEG_INF, cand)
        picked = jnp.where(sel, 1.0, picked)
        tops = jnp.where(r_iota == r, m, tops)
        return cand, picked, tops

    init = (cand, jnp.zeros(cand.shape, F32), jnp.zeros((PEER_TOPK, n), F32))
    _, picked, tops = lax.fori_loop(0, PEER_TOPK, body, init)
    lam_rows = []
    off = 0
    for i in range(SUBLANES):
        rows = SUBLANES * ((_CAND_COLS[i] + SUBLANES - 1) // SUBLANES)
        lam_rows.append(jnp.sum(picked[off:off + rows, :], axis=0, keepdims=True))
        off += rows
    lam_rows.append(picked[off:off + SUBLANES, :])
    return jnp.concatenate(lam_rows, axis=0), tops


def _route_body(x_ref, mod_ref, wq_ref, k1_ref, k2_ref, xt_ref, r2_ref, e2_ref, l_ref, e1_ref,
                sc_scr, rank_scr, vals_scr):
    tb = x_ref.shape[0]
    u = _modulate(x_ref[...], mod_ref[0, 3:4, :], mod_ref[0, 4:5, :])
    xt_ref[...] = u.T.astype(BF16)
    q = jnp.dot(u.astype(BF16), wq_ref[...], preferred_element_type=F32)
    half = PEER_QD // 2
    for hs in range(2 * PEER_HEADS):
        keys = k1_ref[...] if hs % 2 == 0 else k2_ref[...]
        qs = q[:, hs * half:(hs + 1) * half].astype(BF16)
        sc_scr[hs] = lax.dot_general(keys, qs, (((1,), (1,)), ((), ())), preferred_element_type=F32)

    n_chunks = tb // LANES

    def stage1(hs, _):
        for c in range(n_chunks):
            cols = slice(c * LANES, (c + 1) * LANES)
            rank, vals = _top16_rows(sc_scr[hs, :, cols])
            rank_scr[hs, :, cols] = rank
            vals_scr[hs, :, cols] = vals
        return 0

    lax.fori_loop(0, 2 * PEER_HEADS, stage1, 0)

    def stage2(h, _):
        for c in range(n_chunks):
            cols = slice(c * LANES, (c + 1) * LANES)
            v1 = vals_scr[2 * h, :, cols]
            v2 = vals_scr[2 * h + 1, :, cols]
            lam, tops = _select_pairs(v1, v2)
            z = jnp.sum(jnp.exp(tops - tops[0:1, :]), axis=0, keepdims=True)
            rank1 = rank_scr[2 * h, :, cols]
            cnt = jnp.zeros(rank1.shape, F32)
            for i in range(PEER_TOPK):
                cnt = jnp.where(rank1 == float(i), lam[i:i + 1, :], cnt)
            l_ref[h, :, cols] = cnt
            e1_ref[h, :, cols] = jnp.exp(sc_scr[2 * h, :, cols] - v1[0:1, :]) / z
            e2_ref[h, :, cols] = jnp.exp(sc_scr[2 * h + 1, :, cols] - v2[0:1, :])
            r2_ref[h, :, cols] = rank_scr[2 * h + 1, :, cols]
        return 0

    lax.fori_loop(0, PEER_HEADS, stage2, 0)


def _peer_route(x, mod, seq, wq, k1, k2, tb=256):
    n_tok = x.shape[0]
    assert mod.shape[0] == 1 or seq % tb == 0
    mod_b = (lambda i: (i * tb) // seq) if mod.shape[0] > 1 else (lambda i: 0)
    route =pl.BlockSpec((PEER_HEADS, N_KEYS, tb), lambda i: (0, 0, i))
    route_shape = jax.ShapeDtypeStruct((PEER_HEADS, N_KEYS, n_tok), F32)
    return pl.pallas_call(
        _route_body,
        grid=(n_tok // tb,),
        in_specs=[
            pl.BlockSpec((tb, D_MODEL), lambda i: (i, 0)),
            pl.BlockSpec((1, 6, D_MODEL), lambda i: (mod_b(i), 0, 0)),
            pl.BlockSpec((D_MODEL, PEER_HEADS * PEER_QD), lambda i: (0, 0)),
            pl.BlockSpec((N_KEYS, PEER_QD // 2), lambda i: (0, 0)),
            pl.BlockSpec((N_KEYS, PEER_QD // 2), lambda i: (0, 0)),
        ],
        out_specs=[pl.BlockSpec((D_MODEL, tb), lambda i: (0, i)), route, route, route, route],
        out_shape=[jax.ShapeDtypeStruct((D_MODEL, n_tok), BF16)] + [route_shape] * 4,
        scratch_shapes=[
            pltpu.VMEM((2 * PEER_HEADS, N_KEYS, tb), F32),
            pltpu.VMEM((2 * PEER_HEADS, N_KEYS, tb), F32),
            pltpu.VMEM((2 * PEER_HEADS, PEER_TOPK, tb), F32),
        ],
        compiler_params=_cparams(("arbitrary",)),
        name="peer_route",
    )(x, mod, wq, k1, k2)


def _peer_dense_body(xt_ref, u_ref, v_ref, r2_ref, e2_ref, l_ref, e1_ref, x_ref, mod_ref, g_ref, b_ref, o_ref,
                     acc_scr, pt_scr, *, ec):
    j = pl.program_id(1)
    eb = u_ref.shape[0]

    @pl.when(j == 0)
    def _():
        acc_scr[...] = jnp.zeros(acc_scr.shape, F32)

    xt = xt_ref[...]
    a_per_chunk = ec // N_KEYS
    for c in range(eb // ec):
        act = jax.nn.gelu(jnp.dot(u_ref[c * ec:(c + 1) * ec, :], xt, preferred_element_type=F32))
        rows = []
        for al in range(a_per_chunk):
            a = c * a_per_chunk + al
            w = None
            for h in range(PEER_HEADS):
                term = jnp.where(r2_ref[h] < l_ref[h, a:a + 1, :], e2_ref[h], 0.0) * e1_ref[h, a:a + 1, :]
                w = term if w is None else w + term
            rows.append(w)
        weight = jnp.concatenate(rows, axis=0)
        pt_scr[:, c * ec:(c + 1) * ec] = (weight * act).T.astype(BF16)
    acc_scr[...] += jnp.dot(pt_scr[...], v_ref[...], preferred_element_type=F32)

    @pl.when(j == pl.num_programs(1) - 1)
    def _():
        z = ALPHA * x_ref[...] + mod_ref[0, 5:6, :] * acc_scr[...]
        o_ref[...] = _layer_norm(z, g_ref[...], b_ref[...])


def _peer_dense(xt, u_tab, v_tab, r2, e2, l, e1, x, mod, seq, ln_g, ln_b, tb=512, eb=1024, ec=256):
    n_tok = x.shape[0]
    assert mod.shape[0] == 1 or seq % tb == 0
    mod_b = (lambda i: (i * tb) // seq) if mod.shape[0] > 1 else (lambda i: 0)
    route_full =pl.BlockSpec((PEER_HEADS, N_KEYS, tb), lambda i, j: (0, 0, i))
    route_blk = pl.BlockSpec((PEER_HEADS, eb // N_KEYS, tb), lambda i, j: (0, j, i))
    vec = pl.BlockSpec((1, D_MODEL), lambda i, j: (0, 0))
    return pl.pallas_call(
        functools.partial(_peer_dense_body, ec=ec),
        grid=(n_tok // tb, N_EXPERTS // eb),
        in_specs=[
            pl.BlockSpec((D_MODEL, tb), lambda i, j: (0, i)),
            pl.BlockSpec((eb, D_MODEL), lambda i, j: (j, 0)),
            pl.BlockSpec((eb, D_MODEL), lambda i, j: (j, 0)),
            route_full, route_full, route_blk, route_blk,
            pl.BlockSpec((tb, D_MODEL), lambda i, j: (i, 0)),
            pl.BlockSpec((1, 6, D_MODEL), lambda i, j: (mod_b(i), 0, 0)),
            vec, vec,
        ],
        out_specs=pl.BlockSpec((tb, D_MODEL), lambda i, j: (i, 0)),
        out_shape=jax.ShapeDtypeStruct((n_tok, D_MODEL), F32),
        scratch_shapes=[pltpu.VMEM((tb, D_MODEL), F32), pltpu.VMEM((tb, eb), BF16)],
        compiler_params=_cparams(("arbitrary", "arbitrary")),
        name="peer_dense",
    )(xt, u_tab, v_tab, r2, e2, l, e1, x, mod, ln_g.reshape(1, D_MODEL), ln_b.reshape(1, D_MODEL))


def _peer_layer(x, mod, wq, k1, k2, u_tab, v_tab, ln_g, ln_b):
    bsz, seq, _ = x.shape
    xf = x.reshape(bsz * seq, D_MODEL)
    xt, r2, e2, l, e1 = _peer_route(xf, mod, seq, wq, k1, k2)
    out = _peer_dense(xt, u_tab, v_tab, r2, e2, l, e1, xf, mod, seq, ln_g, ln_b)
    return out.reshape(bsz, seq, D_MODEL)


def _trunk(x, mod, cache_k, cache_v, state, w):
    is_ctx = cache_k is None
    bsz = x.shape[0]
    q, k, v, p = _even_in(x, mod[0], w["even_w_in"], rope=not is_ctx)
    if is_ctx:
        attn = _attn_ctx(q, k, v, w["attn_sink"])
    else:
        attn = _attn_lat(q, k, v, cache_k, cache_v, w["attn_sink"])
    pooled = _pool(p, w["pool_w"], w["pool_scale"])
    x = _even_out(attn, pooled, x, mod[0], w["even_w_out"], w["ln1_g"][0], w["ln1_b"][0])
    x = _peer_layer(x, mod[0], w["peer_wq"][0], w["peer_k1"][0], w["peer_k2"][0], w["peer_u"][0], w["peer_v"][0],
                    w["ln2_g"][0], w["ln2_b"][0])
    h0 = jnp.zeros((bsz, 2, LRU_W), F32) if is_ctx else state
    x, fin = _odd_layer(x, mod[1], h0, w["odd_w_r"], w["odd_w_g"], w["conv_w"], w["conv_b"], w["gate_w"],
                        w["gate_a_b"], w["gate_x_b"], w["lru_lambda"], w["odd_w_out"], w["ln1_g"][1], w["ln1_b"][1])
    x = _peer_layer(x, mod[1], w["peer_wq"][1], w["peer_k1"][1], w["peer_k2"][1], w["peer_u"][1], w["peer_v"][1],
                    w["ln2_g"][1], w["ln2_b"][1])
    return x, k, v, fin


def kernel(x_prompt, x_sample, cache_attn_k, cache_attn_v, state_lru, c, c_ctx, ada_w, ada_b, ln1_g, ln1_b, ln2_g,
           ln2_b, even_w_in, attn_sink, pool_w, pool_scale, even_w_out, odd_w_in, conv_w, conv_b, gate_a_w, gate_a_b,
           gate_x_w, gate_x_b, lru_lambda, odd_w_out, peer_wq, peer_k1, peer_k2, peer_u, peer_v):
    batch, seq, _ = x_prompt.shape
    dec_batch = x_sample.shape[0]
    past = cache_attn_k.shape[2]
    w = dict(
        even_w_in=even_w_in[0].astype(BF16), even_w_out=even_w_out[0].astype(BF16), attn_sink=attn_sink[0],
        pool_w=pool_w[0].astype(BF16), pool_scale=pool_scale[0],
        odd_w_r=odd_w_in[0, :, :LRU_W].astype(BF16), odd_w_g=odd_w_in[0, :, LRU_W:].astype(BF16),
        conv_w=conv_w[0], conv_b=conv_b[0],
        gate_w=jnp.concatenate([gate_a_w[0], gate_x_w[0]], axis=-1).astype(BF16),
        gate_a_b=gate_a_b[0], gate_x_b=gate_x_b[0], lru_lambda=lru_lambda[0], odd_w_out=odd_w_out[0].astype(BF16),
        peer_wq=peer_wq.astype(BF16), peer_k1=peer_k1.astype(BF16), peer_k2=peer_k2.astype(BF16),
        peer_u=peer_u.astype(BF16), peer_v=peer_v.astype(BF16),
        ln1_g=ln1_g, ln1_b=ln1_b, ln2_g=ln2_g, ln2_b=ln2_b,
    )
    rows = SUBLANES * ((1 + dec_batch + SUBLANES - 1) // SUBLANES)
    cond = jnp.zeros((rows, D_MODEL), F32).at[0].set(c_ctx).at[1:1 + dec_batch].set(c)
    mod = _ada(cond, ada_w, ada_b)
    mod_ctx = mod[:, 0:1]
    mod_lat = mod[:, 1:1 + dec_batch]

    y_prompt, k_ctx, v_ctx, fin = _trunk(x_prompt, mod_ctx, None, None, None, w)
    ck = cache_attn_k[:, 0].reshape(dec_batch, past, KV_W)
    cv = cache_attn_v[:, 0].reshape(dec_batch, past, KV_W)
    y_sample, _, _, _ = _trunk(x_sample, mod_lat, ck, cv, state_lru[:, 0], w)

    new_k = k_ctx.reshape(batch, 1, seq, N_KV_HEADS, HEAD_DIM)
    new_v = v_ctx.reshape(batch, 1, seq, N_KV_HEADS, HEAD_DIM)
    new_state = fin.reshape(batch, 1, 2, LRU_W)
    return (y_prompt, y_sample, new_k, new_v, new_state)
```

```python
import functools
import math

import jax
import jax.numpy as jnp
from jax import lax
from jax.experimental import pallas as pl
from jax.experimental.pallas import tpu as pltpu

F32 = jnp.float32
BF16 = jnp.bfloat16

D_MODEL = 1024
DEPTH = 2
GRID_W = 64
N_Q_HEADS = 8
N_KV_HEADS = 2
GROUP = N_Q_HEADS // N_KV_HEADS
HEAD_DIM = 64
ATTN_W = N_Q_HEADS * HEAD_DIM
KV_W = N_KV_HEADS * HEAD_DIM
ATTN_BLOCK = 128
ATTN_SCALE = HEAD_DIM ** -0.5
ROPE_BASE = 10000.0
POOL_WINDOWS = (2, 4, 8, 16)
POOL_W = D_MODEL // 2
POOL_GW = POOL_W // len(POOL_WINDOWS)
POOL_HALO = 8
EVEN_IN = ATTN_W + 2 * KV_W + POOL_W
LRU_W = D_MODEL
LRU_HEADS = 8
LRU_HD = LRU_W // LRU_HEADS
CONV_W = 4
CONV_HALO = 8
LRU_C = 8.0
N_KEYS = 128
N_EXPERTS = N_KEYS * N_KEYS
PEER_HEADS = 8
PEER_QD = 256
PEER_TOPK = 16
ALPHA = (2 * DEPTH) ** 0.25
LN_EPS = 1e-5
NEG_INF = float("-inf")

LANES = 128
SUBLANES = 8
VMEM_LIMIT_BYTES = 56 * 1024 * 1024

_CAND_COLS = tuple(min(PEER_TOPK, PEER_TOPK // (i + 1)) for i in range(PEER_TOPK))


def _cparams(sem):
    return pltpu.CompilerParams(dimension_semantics=sem, vmem_limit_bytes=VMEM_LIMIT_BYTES)


def _modulate(x, shift, scale):
    return x * (1.0 + scale) + shift


def _layer_norm(z, g, b):
    mu = jnp.mean(z, axis=-1, keepdims=True)
    zc = z - mu
    var = jnp.mean(zc * zc, axis=-1, keepdims=True)
    return zc * lax.rsqrt(var + LN_EPS) * g + b


def _ada_body(c_ref, w_ref, b_ref, o_ref):
    c = c_ref[...]
    s = (c * jax.nn.sigmoid(c)).astype(BF16)
    o_ref[0] = jnp.dot(s, w_ref[0].astype(BF16), preferred_element_type=F32) + b_ref[0]


def _ada(cond, ada_w, ada_b):
    rows = cond.shape[0]
    n_out = 6 * D_MODEL
    nb = 1536
    out = pl.pallas_call(
        _ada_body,
        grid=(DEPTH, n_out // nb),
        in_specs=[
            pl.BlockSpec((rows, D_MODEL), lambda l, n: (0, 0)),
            pl.BlockSpec((1, D_MODEL, nb), lambda l, n: (l, 0, n)),
            pl.BlockSpec((1, 1, nb), lambda l, n: (l, 0, n)),
        ],
        out_specs=pl.BlockSpec((1, rows, nb), lambda l, n: (l, 0, n)),
        out_shape=jax.ShapeDtypeStruct((DEPTH, rows, n_out), F32),
        compiler_params=_cparams(("arbitrary", "arbitrary")),
        name="ada_mod",
    )(cond, ada_w, ada_b.reshape(DEPTH, 1, n_out))
    return out.reshape(DEPTH, rows, 6, D_MODEL)


def _rope_tables(seq):
    rows = seq // GRID_W
    row = jnp.repeat(jnp.arange(rows, dtype=F32), GRID_W)
    col = (jnp.arange(rows * GRID_W) % GRID_W).astype(F32)
    n_freq = HEAD_DIM // 4
    inv = ROPE_BASE ** (-jnp.arange(n_freq, dtype=F32) / n_freq)
    ang = jnp.concatenate([row[:, None] * inv, col[:, None] * inv], axis=-1)
    c, s = jnp.cos(ang), jnp.sin(ang)
    cos = jnp.concatenate([c, c, c, c], axis=-1)
    sin = jnp.concatenate([-s, s, -s, s], axis=-1)
    return cos, sin


def _rope128(x, cos, sin, lane_lo):
    rot = jnp.where(lane_lo, pltpu.roll(x, LANES - HEAD_DIM // 2, 1), pltpu.roll(x, HEAD_DIM // 2, 1))
    return x * cos + rot * sin


def _even_in_body(*refs, rope):
    if rope:
        x_ref, mod_ref, w_ref, cos_ref, sin_ref, q_ref, k_ref, v_ref, p_ref = refs
    else:
        x_ref, mod_ref, w_ref, q_ref, k_ref, v_ref, p_ref = refs
    u = _modulate(x_ref[0], mod_ref[0, 0:1, :], mod_ref[0, 1:2, :]).astype(BF16)
    h = jnp.dot(u, w_ref[...], preferred_element_type=F32)
    q = h[:, :ATTN_W]
    k = h[:, ATTN_W:ATTN_W + KV_W]
    if rope:
        cos = cos_ref[...]
        sin = sin_ref[...]
        lane = lax.broadcasted_iota(jnp.int32, cos.shape, 1)
        lane_lo = (lane % HEAD_DIM) < (HEAD_DIM // 2)
        q = jnp.concatenate(
            [_rope128(q[:, c * LANES:(c + 1) * LANES], cos, sin, lane_lo) for c in range(ATTN_W // LANES)], axis=1)
        k = _rope128(k, cos, sin, lane_lo)
    q_ref[0] = q.astype(BF16)
    k_ref[0] = k
    v_ref[0] = h[:, ATTN_W + KV_W:ATTN_W + 2 * KV_W]
    p_ref[0] = h[:, ATTN_W + 2 * KV_W:]


def _even_in(x, mod, w_in, rope, tb=256):
    bsz, seq, _ = x.shape
    mod_b = (lambda b: b) if mod.shape[0] > 1 else (lambda b: 0)
    in_specs = [
        pl.BlockSpec((1, tb, D_MODEL), lambda b, i: (b, i, 0)),
        pl.BlockSpec((1, 6, D_MODEL), lambda b, i: (mod_b(b), 0, 0)),
        pl.BlockSpec((D_MODEL, EVEN_IN), lambda b, i: (0, 0)),
    ]
    args = [x, mod, w_in]
    if rope:
        cos, sin = _rope_tables(seq)
        in_specs += [pl.BlockSpec((tb, LANES), lambda b, i: (i, 0))] * 2
        args += [cos, sin]
    return pl.pallas_call(
        functools.partial(_even_in_body, rope=rope),
        grid=(bsz, seq // tb),
        in_specs=in_specs,
        out_specs=[
            pl.BlockSpec((1, tb, ATTN_W), lambda b, i: (b, i, 0)),
            pl.BlockSpec((1, tb, KV_W), lambda b, i: (b, i, 0)),
            pl.BlockSpec((1, tb, KV_W), lambda b, i: (b, i, 0)),
            pl.BlockSpec((1, tb, POOL_W), lambda b, i: (b, i, 0)),
        ],
        out_shape=[
            jax.ShapeDtypeStruct((bsz, seq, ATTN_W), BF16),
            jax.ShapeDtypeStruct((bsz, seq, KV_W), F32),
            jax.ShapeDtypeStruct((bsz, seq, KV_W), F32),
            jax.ShapeDtypeStruct((bsz, seq, POOL_W), F32),
        ],
        compiler_params=_cparams(("arbitrary", "arbitrary")),
        name="even_in_rope" if rope else "even_in",
    )(*args)


def _attend(q, kcat, vcat, sink_ref, bias, o_ref):
    for kv in range(N_KV_HEADS):
        kk = kcat[:, kv * HEAD_DIM:(kv + 1) * HEAD_DIM]
        vv = vcat[:, kv * HEAD_DIM:(kv + 1) * HEAD_DIM]
        for g in range(GROUP):
            hq = kv * GROUP + g
            qh = q[:, hq * HEAD_DIM:(hq + 1) * HEAD_DIM]
            s = lax.dot_general(qh, kk, (((1,), (1,)), ((), ())), preferred_element_type=F32) * ATTN_SCALE
            if bias is not None:
                s = s + bias
            sk = sink_ref[hq]
            m = jnp.maximum(jnp.max(s, axis=-1, keepdims=True), sk)
            p = jnp.exp(s - m)
            den = jnp.sum(p, axis=-1, keepdims=True) + jnp.exp(sk - m)
            o = jnp.dot((p / den).astype(BF16), vv, preferred_element_type=F32)
            o_ref[0, :, hq * HEAD_DIM:(hq + 1) * HEAD_DIM] = o


def _attn_ctx_body(sink_ref, q_ref, k_ref, v_ref, o_ref):
    _attend(q_ref[0], k_ref[0].astype(BF16), v_ref[0].astype(BF16), sink_ref, None, o_ref)


def _attn_ctx(q, k, v, sink):
    bsz, seq, _ = q.shape
    blk = ATTN_BLOCK
    return pl.pallas_call(
        _attn_ctx_body,
        grid=(bsz, seq // blk),
        in_specs=[
            pl.BlockSpec(memory_space=pltpu.SMEM),
            pl.BlockSpec((1, blk, ATTN_W), lambda b, n: (b, n, 0)),
            pl.BlockSpec((1, seq, KV_W), lambda b, n: (b, 0, 0)),
            pl.BlockSpec((1, seq, KV_W), lambda b, n: (b, 0, 0)),
        ],
        out_specs=pl.BlockSpec((1, blk, ATTN_W), lambda b, n: (b, n, 0)),
        out_shape=jax.ShapeDtypeStruct((bsz, seq, ATTN_W), F32),
        compiler_params=_cparams(("arbitrary", "arbitrary")),
        name="attn_ctx",
    )(sink, q, k, v)


def _attn_lat_body(sink_ref, q_ref, kp_ref, kc_ref, kn_ref, vp_ref, vc_ref, vn_ref, ck_ref, cv_ref, o_ref):
    n = pl.program_id(1)
    nb = pl.num_programs(1)
    blk = ATTN_BLOCK
    qi = lax.broadcasted_iota(jnp.int32, (blk, blk), 0)
    kj = lax.broadcasted_iota(jnp.int32, (blk, blk), 1)
    prev_ok = jnp.where(n > 0, 0.0, NEG_INF)
    next_ok = jnp.where(n < nb - 1, 0.0, NEG_INF)
    bias_prev = jnp.where(kj >= qi, 0.0, NEG_INF) + prev_ok
    bias_next = jnp.where(kj <= qi, 0.0, NEG_INF) + next_ok
    n_ctx = ck_ref.shape[1]
    bias = jnp.concatenate(
        [bias_prev, jnp.zeros((blk, blk), F32), bias_next, jnp.zeros((blk, n_ctx), F32)], axis=1)
    kcat = jnp.concatenate([kp_ref[0], kc_ref[0], kn_ref[0], ck_ref[0]], axis=0).astype(BF16)
    vcat = jnp.concatenate([vp_ref[0], vc_ref[0], vn_ref[0], cv_ref[0]], axis=0).astype(BF16)
    _attend(q_ref[0], kcat, vcat, sink_ref, bias, o_ref)


def _attn_lat(q, k, v, ck, cv, sink):
    bsz, seq, _ = q.shape
    blk = ATTN_BLOCK
    nb = seq // blk
    n_ctx = ck.shape[1]
    prev = lambda b, n: (b, jnp.maximum(n - 1, 0), 0)
    cur = lambda b, n: (b, n, 0)
    nxt = lambda b, n: (b, jnp.minimum(n + 1, nb - 1), 0)
    kv_spec = lambda im: pl.BlockSpec((1, blk, KV_W), im)
    ctx_spec = pl.BlockSpec((1, n_ctx, KV_W), lambda b, n: (b, 0, 0))
    return pl.pallas_call(
        _attn_lat_body,
        grid=(bsz, nb),
        in_specs=[
            pl.BlockSpec(memory_space=pltpu.SMEM),
            pl.BlockSpec((1, blk, ATTN_W), cur),
            kv_spec(prev), kv_spec(cur), kv_spec(nxt),
            kv_spec(prev), kv_spec(cur), kv_spec(nxt),
            ctx_spec, ctx_spec,
        ],
        out_specs=pl.BlockSpec((1, blk, ATTN_W), cur),
        out_shape=jax.ShapeDtypeStruct((bsz, seq, ATTN_W), F32),
        compiler_params=_cparams(("arbitrary", "arbitrary")),
        name="attn_lat",
    )(sink, q, k, k, k, v, v, v, ck, cv)


def _pool_body(p_ref, w_ref, sc_ref, o_ref, pad_ref):
    seq = p_ref.shape[1]
    halo = POOL_HALO
    t = lax.broadcasted_iota(jnp.int32, (seq, POOL_GW), 0)
    pad_ref[0:halo, :] = jnp.zeros((halo, POOL_GW), F32)
    pad_ref[halo + seq:2 * halo + seq, :] = jnp.zeros((halo, POOL_GW), F32)
    for g, w in enumerate(POOL_WINDOWS):
        lo = w // 2
        hi = w - lo - 1
        cols = slice(g * POOL_GW, (g + 1) * POOL_GW)
        pg = p_ref[0, :, cols]
        pad_ref[halo:halo + seq, :] = pg
        acc = pad_ref[halo - lo:halo - lo + seq, :]
        for d in range(-lo + 1, hi + 1):
            acc = acc + pad_ref[halo + d:halo + d + seq, :]
        cnt = (jnp.minimum(t + hi + 1, seq) - jnp.maximum(t - lo, 0)).astype(F32)
        pooled = (acc / cnt - pg).astype(BF16)
        y = jnp.dot(pooled, w_ref[g], preferred_element_type=F32)
        o_ref[0, :, cols] = y * sc_ref[:, cols]


def _pool(p, pool_w, pool_scale):
    bsz, seq, _ = p.shape
    n_groups = len(POOL_WINDOWS)
    return pl.pallas_call(
        _pool_body,
        grid=(bsz,),
        in_specs=[
            pl.BlockSpec((1, seq, POOL_W), lambda b: (b, 0, 0)),
            pl.BlockSpec((n_groups, POOL_GW, POOL_GW), lambda b: (0, 0, 0)),
            pl.BlockSpec((1, POOL_W), lambda b: (0, 0)),
        ],
        out_specs=pl.BlockSpec((1, seq, POOL_W), lambda b: (b, 0, 0)),
        out_shape=jax.ShapeDtypeStruct((bsz, seq, POOL_W), F32),
        scratch_shapes=[pltpu.VMEM((seq + 2 * POOL_HALO, POOL_GW), F32)],
        compiler_params=_cparams(("arbitrary",)),
        name="pool",
    )(p, pool_w, pool_scale.reshape(1, POOL_W))


def _even_out_body(attn_ref, pool_ref, x_ref, mod_ref, w_ref, g_ref, b_ref, o_ref):
    cat = jnp.concatenate([attn_ref[0], pool_ref[0]], axis=1).astype(BF16)
    y = jnp.dot(cat, w_ref[...], preferred_element_type=F32)
    z = ALPHA * x_ref[0] + mod_ref[0, 2:3, :] * y
    o_ref[0] = _layer_norm(z, g_ref[...], b_ref[...])


def _even_out(attn, pooled, x, mod, w_out, ln_g, ln_b, tb=256):
    bsz, seq, _ = x.shape
    mod_b = (lambda b: b) if mod.shape[0] > 1 else (lambda b: 0)
    tok = lambda w: pl.BlockSpec((1, tb, w), lambda b, i: (b, i, 0))
    vec = pl.BlockSpec((1, D_MODEL), lambda b, i: (0, 0))
    return pl.pallas_call(
        _even_out_body,
        grid=(bsz, seq // tb),
        in_specs=[
            tok(ATTN_W), tok(POOL_W), tok(D_MODEL),
            pl.BlockSpec((1, 6, D_MODEL), lambda b, i: (mod_b(b), 0, 0)),
            pl.BlockSpec((ATTN_W + POOL_W, D_MODEL), lambda b, i: (0, 0)),
            vec, vec,
        ],
        out_specs=tok(D_MODEL),
        out_shape=jax.ShapeDtypeStruct((bsz, seq, D_MODEL), F32),
        compiler_params=_cparams(("arbitrary", "arbitrary")),
        name="even_out",
    )(attn, pooled, x, mod, w_out, ln_g.reshape(1, D_MODEL), ln_b.reshape(1, D_MODEL))


def _softplus(z):
    return jnp.maximum(z, 0.0) + jnp.log1p(jnp.exp(-jnp.abs(z)))


def _lru_coeffs(xc, gw_ref, ba, bx, lam, a_scr, b_scr):
    sp = _softplus(-lam)
    for hh in range(LRU_HEADS):
        cols = slice(hh * LRU_HD, (hh + 1) * LRU_HD)
        xh = xc[:, cols]
        pre = jnp.dot(xh.astype(BF16), gw_ref[hh], preferred_element_type=F32)
        r = jax.nn.sigmoid(pre[:, :LRU_HD] + ba[:, cols])
        ig = jax.nn.sigmoid(pre[:, LRU_HD:] + bx[:, cols])
        log_a = -LRU_C * r * sp[:, cols]
        a = jnp.exp(log_a)
        b = jnp.sqrt(jnp.tanh(-log_a) * (a * a + 1.0)) * (ig * xh)
        a_scr[:, cols] = a
        b_scr[:, cols] = b


def _conv_input(x_ref, xp_ref, xn_ref, mod_ref, wr_ref, cw_ref, cb_ref, xr_scr, blk, nblk):
    tb = x_ref.shape[1]
    halo = CONV_HALO
    shift = mod_ref[0, 0:1, :]
    scale = mod_ref[0, 1:2, :]
    um = _modulate(x_ref[0], shift, scale).astype(BF16)
    up = _modulate(xp_ref[0], shift, scale).astype(BF16)
    un = _modulate(xn_ref[0], shift, scale).astype(BF16)
    w = wr_ref[...]
    has_prev = jnp.where(blk > 0, 1.0, 0.0)
    has_next = jnp.where(blk < nblk - 1, 1.0, 0.0)
    xr_scr[0:halo, :] = jnp.dot(up, w, preferred_element_type=F32) * has_prev
    xr_scr[halo:halo + tb, :] = jnp.dot(um, w, preferred_element_type=F32)
    xr_scr[halo + tb:2 * halo + tb, :] = jnp.dot(un, w, preferred_element_type=F32) * has_next
    left = CONV_W // 2
    xc = cb_ref[...] + xr_scr[halo - left:halo - left + tb, :] * cw_ref[0:1, :]
    for j in range(1, CONV_W):
        xc = xc + xr_scr[halo - left + j:halo - left + j + tb, :] * cw_ref[j:j + 1, :]
    return um, xc


def _odd_fwd_body(x_ref, xp_ref, xn_ref, mod_ref, wr_ref, cw_ref, cb_ref, gw_ref, ba_ref, bx_ref, lam_ref,
                  h0_ref, hf_ref, fin_ref, xr_scr, a_scr, b_scr, h_scr):
    i = pl.program_id(1)
    nblk = pl.num_programs(1)
    tb = x_ref.shape[1]
    _, xc = _conv_input(x_ref, xp_ref, xn_ref, mod_ref, wr_ref, cw_ref, cb_ref, xr_scr, i, nblk)
    _lru_coeffs(xc, gw_ref, ba_ref[...], bx_ref[...], lam_ref[...], a_scr, b_scr)

    @pl.when(i == 0)
    def _():
        h_scr[...] = h0_ref[0, 0]

    def tile(t8, h):
        off = pl.multiple_of(t8 * SUBLANES, SUBLANES)
        a8 = a_scr[pl.ds(off, SUBLANES), :]
        b8 = b_scr[pl.ds(off, SUBLANES), :]
        rows = []
        for r in range(SUBLANES):
            h = a8[r:r + 1, :] * h + b8[r:r + 1, :]
            rows.append(h)
        hf_ref[0, pl.ds(off, SUBLANES), :] = jnp.concatenate(rows, axis=0)
        return h

    h = lax.fori_loop(0, tb // SUBLANES, tile, h_scr[...])
    h_scr[...] = h
    fin_ref[0] = h


def _odd_bwd_body(x_ref, xp_ref, xn_ref, mod_ref, wr_ref, wg_ref, cw_ref, cb_ref, gw_ref, ba_ref, bx_ref, lam_ref,
                  h0_ref, hf_ref, wo_ref, g_ref, b_ref, o_ref, fin_ref, xr_scr, a_scr, b_scr, h_scr, hs_scr):
    i = pl.program_id(1)
    nblk = pl.num_programs(1)
    blk = nblk - 1 - i
    tb = x_ref.shape[1]
    um, xc = _conv_input(x_ref, xp_ref, xn_ref, mod_ref, wr_ref, cw_ref, cb_ref, xr_scr, blk, nblk)
    _lru_coeffs(xc, gw_ref, ba_ref[...], bx_ref[...], lam_ref[...], a_scr, b_scr)

    @pl.when(i == 0)
    def _():
        h_scr[...] = h0_ref[0, 0]

    n_tiles = tb // SUBLANES

    def tile(k, h):
        off = pl.multiple_of((n_tiles - 1 - k) * SUBLANES, SUBLANES)
        a8 = a_scr[pl.ds(off, SUBLANES), :]
        b8 = b_scr[pl.ds(off, SUBLANES), :]
        rows = []
        for r in reversed(range(SUBLANES)):
            h = a8[r:r + 1, :] * h + b8[r:r + 1, :]
            rows.append(h)
        hb8 = jnp.concatenate(rows[::-1], axis=0)
        hs_scr[pl.ds(off, SUBLANES), :] = hf_ref[0, pl.ds(off, SUBLANES), :] + hb8
        return h

    h = lax.fori_loop(0, n_tiles, tile, h_scr[...])
    h_scr[...] = h
    fin_ref[0] = h
    xg = jnp.dot(um, wg_ref[...], preferred_element_type=F32)
    gated = (hs_scr[...] * jax.nn.gelu(xg)).astype(BF16)
    y = jnp.dot(gated, wo_ref[...], preferred_element_type=F32)
    z = ALPHA * x_ref[0] + mod_ref[0, 2:3, :] * y
    o_ref[0] = _layer_norm(z, g_ref[...], b_ref[...])


def _odd_layer(x, mod, h0, w_r, w_g, conv_w, conv_b, gw, ba, bx, lam, w_out, ln_g, ln_b, tb=256):
    bsz, seq, _ = x.shape
    nblk = seq // tb
    rows8 = seq // CONV_HALO
    per8 = tb // CONV_HALO
    mod_b = (lambda b: b) if mod.shape[0] > 1 else (lambda b: 0)
    vec = lambda im=None: pl.BlockSpec((1, D_MODEL), lambda b, i: (0, 0))
    full = lambda shape: pl.BlockSpec(shape, lambda b, i: tuple(0 for _ in shape))

    def x_specs(blk_of):
        return [
            pl.BlockSpec((1, tb, D_MODEL), lambda b, i: (b, blk_of(i), 0)),
            pl.BlockSpec((1, CONV_HALO, D_MODEL), lambda b, i: (b, jnp.maximum(blk_of(i) * per8 - 1, 0), 0)),
            pl.BlockSpec((1, CONV_HALO, D_MODEL), lambda b, i: (b, jnp.minimum((blk_of(i) + 1) * per8, rows8 - 1), 0)),
            pl.BlockSpec((1, 6, D_MODEL), lambda b, i: (mod_b(b), 0, 0)),
        ]

    gate_specs = [
        full((LRU_HEADS, LRU_HD, 2 * LRU_HD)),
        vec(), vec(), vec(),
    ]
    scan_scratch = [
        pltpu.VMEM((tb + 2 * CONV_HALO, LRU_W), F32),
        pltpu.VMEM((tb, LRU_W), F32),
        pltpu.VMEM((tb, LRU_W), F32),
        pltpu.VMEM((1, LRU_W), F32),
    ]
    row = lambda a: a.reshape(1, D_MODEL)
    state_spec = lambda d: pl.BlockSpec((1, 1, 1, D_MODEL), lambda b, i: (b, d, 0, 0))
    h0 = h0.reshape(bsz, 2, 1, LRU_W)

    fwd_blk = lambda i: i
    hf, fin_f = pl.pallas_call(
        _odd_fwd_body,
        grid=(bsz, nblk),
        in_specs=x_specs(fwd_blk) + [full((D_MODEL, LRU_W)), full((CONV_W, LRU_W)), vec()] + gate_specs
        + [state_spec(0)],
        out_specs=[
            pl.BlockSpec((1, tb, LRU_W), lambda b, i: (b, i, 0)),
            pl.BlockSpec((1, 1, LRU_W), lambda b, i: (b, 0, 0)),
        ],
        out_shape=[
            jax.ShapeDtypeStruct((bsz, seq, LRU_W), F32),
            jax.ShapeDtypeStruct((bsz, 1, LRU_W), F32),
        ],
        scratch_shapes=scan_scratch,
        compiler_params=_cparams(("arbitrary", "arbitrary")),
        name="lru_fwd",
    )(x, x, x, mod, w_r, conv_w, row(conv_b), gw[0], row(ba[0]), row(bx[0]), row(lam[0]), h0)

    bwd_blk = lambda i: nblk - 1 - i
    x1, fin_b = pl.pallas_call(
        _odd_bwd_body,
        grid=(bsz, nblk),
        in_specs=x_specs(bwd_blk) + [full((D_MODEL, LRU_W)), full((D_MODEL, LRU_W)), full((CONV_W, LRU_W)), vec()]
        + gate_specs + [
            state_spec(1),
            pl.BlockSpec((1, tb, LRU_W), lambda b, i: (b, bwd_blk(i), 0)),
            full((LRU_W, D_MODEL)), vec(), vec(),
        ],
        out_specs=[
            pl.BlockSpec((1, tb, D_MODEL), lambda b, i: (b, bwd_blk(i), 0)),
            pl.BlockSpec((1, 1, LRU_W), lambda b, i: (b, 0, 0)),
        ],
        out_shape=[
            jax.ShapeDtypeStruct((bsz, seq, D_MODEL), F32),
            jax.ShapeDtypeStruct((bsz, 1, LRU_W), F32),
        ],
        scratch_shapes=scan_scratch + [pltpu.VMEM((tb, LRU_W), F32)],
        compiler_params=_cparams(("arbitrary", "arbitrary")),
        name="lru_bwd_out",
    )(x, x, x, mod, w_r, w_g, conv_w, row(conv_b), gw[1], row(ba[1]), row(bx[1]), row(lam[1]), h0, hf,
      w_out, row(ln_g), row(ln_b))
    return x1, jnp.concatenate([fin_f, fin_b], axis=1)


_N_CHAINS = 4
_PAIRS = tuple((i, j) for i in range(PEER_TOPK) for j in range(_CAND_COLS[i]))


def _tree(op, xs):
    while len(xs) > 1:
        xs = [op(xs[i], xs[i + 1]) if i + 1 < len(xs) else xs[i] for i in range(0, len(xs), 2)]
    return xs[0]


def _sweep_max(n, visit):
    accs = [None] * _N_CHAINS
    for k in range(n):
        v = visit(k)
        j = k % _N_CHAINS
        accs[j] = v if accs[j] is None else jnp.maximum(accs[j], v)
    return _tree(jnp.maximum, accs)


def _first_index(n, load, m):
    idxs = [jnp.full(m.shape, float(n), F32)] * _N_CHAINS
    for k in reversed(range(n)):
        j = k % _N_CHAINS
        idxs[j] = jnp.where(load(k) == m, float(k), idxs[j])
    return _tree(jnp.minimum, idxs)


def _top16_keys(src_ref, work_ref, rank_ref, vals_ref):
    not_selected = jnp.full(src_ref.shape[1:], float(N_KEYS - 1), F32)

    def init(k):
        v = src_ref[k]
        work_ref[k] = v
        rank_ref[k] = not_selected
        return v

    def body(r, m):
        idx = _first_index(N_KEYS, lambda k: work_ref[k], m)
        vals_ref[r] = m
        rank = r.astype(F32)

        def remove(k):
            sel = idx == float(k)
            v = jnp.where(sel, NEG_INF, work_ref[k])
            work_ref[k] = v
            rank_ref[k] = jnp.where(sel, rank, rank_ref[k])
            return v

        return _sweep_max(N_KEYS, remove)

    lax.fori_loop(0, PEER_TOPK, body, _sweep_max(N_KEYS, init))


def _select_pairs(vals1_ref, vals2_ref, cand_ref, pick_ref):
    n = len(_PAIRS)
    zero = jnp.zeros(cand_ref.shape[1:], F32)

    def init(p):
        i, j = _PAIRS[p]
        c = vals1_ref[i] + vals2_ref[j]
        cand_ref[p] = c
        pick_ref[p] = zero
        return c

    top = _sweep_max(n, init)

    def body(r, carry):
        m, z = carry
        idx = _first_index(n, lambda p: cand_ref[p], m)

        def remove(p):
            sel = idx == float(p)
            c = jnp.where(sel, NEG_INF, cand_ref[p])
            cand_ref[p] = c
            pick_ref[p] = jnp.where(sel, 1.0, pick_ref[p])
            return c

        return _sweep_max(n, remove), z + jnp.exp(m - top)

    _, z = lax.fori_loop(0, PEER_TOPK, body, (top, zero))
    lam = []
    p = 0
    for i in range(PEER_TOPK):
        lam.append(_tree(jnp.add, [pick_ref[p + j] for j in range(_CAND_COLS[i])]))
        p += _CAND_COLS[i]
    return lam, z


def _dup_bf16(x):
    w = pltpu.bitcast(x.astype(BF16).astype(F32), jnp.uint32)
    return w | (w >> 16)


def _route_body(x_ref, mod_ref, wq_ref, k1_ref, k2_ref, xt_ref, r2_ref, e2_ref, l_ref, e1_ref,
                sc_scr, work_scr, rank_scr, vals_scr, cand_scr, pick_scr, stage_scr):
    tb = x_ref.shape[0]
    n_chunks = tb // LANES
    u = _modulate(x_ref[...], mod_ref[0, 3:4, :], mod_ref[0, 4:5, :])
    xt_ref[...] = u.T.astype(BF16)
    q = jnp.dot(u.astype(BF16), wq_ref[...], preferred_element_type=F32)
    half = PEER_QD // 2
    for hs in range(2 * PEER_HEADS):
        h, side = divmod(hs, 2)
        keys = k1_ref[...] if side == 0 else k2_ref[...]
        qs = q[:, hs * half:(hs + 1) * half].astype(BF16)
        st = lax.dot_general(keys, qs, (((1,), (1,)), ((), ())), preferred_element_type=F32)
        for c in range(n_chunks):
            sc_scr[side, c, :, h, :] = st[:, c * LANES:(c + 1) * LANES]

    for c in range(n_chunks):
        cols = slice(c * LANES, (c + 1) * LANES)
        for side in range(2):
            _top16_keys(sc_scr.at[side, c], work_scr, rank_scr.at[side], vals_scr.at[side])
        lam, z = _select_pairs(vals_scr.at[0], vals_scr.at[1], cand_scr, pick_scr)
        inv_z = 1.0 / z
        top1 = vals_scr[0, 0]
        top2 = vals_scr[1, 0]
        for k in range(N_KEYS):
            rank1 = rank_scr[0, k]
            cnt = jnp.zeros(rank1.shape, F32)
            for i in range(PEER_TOPK):
                cnt = jnp.where(rank1 == float(i), lam[i], cnt)
            stage_scr[0, k] = cnt
            stage_scr[1, k] = jnp.exp(sc_scr[0, c, k] - top1) * inv_z
            stage_scr[2, k] = jnp.exp(sc_scr[1, c, k] - top2)
        for h in range(PEER_HEADS):
            l_ref[h, :, cols] = _dup_bf16(stage_scr[0, :, h, :])
            e1_ref[h, :, cols] = _dup_bf16(stage_scr[1, :, h, :])
            e2_ref[h, :, cols] = stage_scr[2, :, h, :].astype(BF16)
            r2_ref[h, :, cols] = rank_scr[1, :, h, :].astype(BF16)


def _peer_route(x, mod, seq, wq, k1, k2, tb=256):
    n_tok = x.shape[0]
    assert mod.shape[0] == 1 or seq % tb == 0
    mod_b = (lambda i: (i * tb) // seq) if mod.shape[0] > 1 else (lambda i: 0)
    route = pl.BlockSpec((PEER_HEADS, N_KEYS, tb), lambda i: (0, 0, i))
    route_shape = lambda dt: jax.ShapeDtypeStruct((PEER_HEADS, N_KEYS, n_tok), dt)
    sweep = (PEER_HEADS, LANES)
    return pl.pallas_call(
        _route_body,
        grid=(n_tok // tb,),
        in_specs=[
            pl.BlockSpec((tb, D_MODEL), lambda i: (i, 0)),
            pl.BlockSpec((1, 6, D_MODEL), lambda i: (mod_b(i), 0, 0)),
            pl.BlockSpec((D_MODEL, PEER_HEADS * PEER_QD), lambda i: (0, 0)),
            pl.BlockSpec((N_KEYS, PEER_QD // 2), lambda i: (0, 0)),
            pl.BlockSpec((N_KEYS, PEER_QD // 2), lambda i: (0, 0)),
        ],
        out_specs=[pl.BlockSpec((D_MODEL, tb), lambda i: (0, i)), route, route, route, route],
        out_shape=[jax.ShapeDtypeStruct((D_MODEL, n_tok), BF16), route_shape(BF16), route_shape(BF16),
                   route_shape(jnp.uint32), route_shape(jnp.uint32)],
        scratch_shapes=[
            pltpu.VMEM((2, tb // LANES, N_KEYS) + sweep, F32),
            pltpu.VMEM((N_KEYS,) + sweep, F32),
            pltpu.VMEM((2, N_KEYS) + sweep, F32),
            pltpu.VMEM((2, PEER_TOPK) + sweep, F32),
            pltpu.VMEM((len(_PAIRS),) + sweep, F32),
            pltpu.VMEM((len(_PAIRS),) + sweep, F32),
            pltpu.VMEM((3, N_KEYS) + sweep, F32),
        ],
        compiler_params=_cparams(("arbitrary",)),
        name="peer_route",
    )(x, mod, wq, k1, k2)


_GELU_C0 = -2.0 * math.sqrt(2.0 / math.pi) * math.log2(math.e)
_GELU_C1 = 0.044715 * _GELU_C0


def _gelu_tanh(x):
    return x / (1.0 + jnp.exp2(x * (_GELU_C0 + _GELU_C1 * (x * x))))


def _peer_dense_body(xt_ref, u_ref, vt_ref, r2_ref, e2_ref, l_ref, e1_ref, x_ref, mod_ref, g_ref, b_ref, o_ref,
                     acc_scr, pt_scr, *, ec):
    j = pl.program_id(1)
    eb, tb = u_ref.shape[0], xt_ref.shape[1]

    @pl.when(j == 0)
    def _():
        acc_scr[...] = jnp.zeros(acc_scr.shape, F32)

    def bcast_rows(ref, h, a):
        return pltpu.bitcast(jnp.broadcast_to(ref[h, a:a + 1, :], (N_KEYS // 2, tb)), BF16)

    xt = xt_ref[...]
    a_per_chunk = ec // N_KEYS
    for c in range(eb // ec):
        act = _gelu_tanh(jnp.dot(u_ref[c * ec:(c + 1) * ec, :], xt, preferred_element_type=F32))
        rows = []
        for al in range(a_per_chunk):
            a = c * a_per_chunk + al
            w = None
            for h in range(PEER_HEADS):
                e2 = e2_ref[h]
                term = jnp.where(r2_ref[h] < bcast_rows(l_ref, h, a), e2, jnp.zeros_like(e2)) * bcast_rows(e1_ref, h, a)
                w = term if w is None else w + term
            rows.append(w)
        pt_scr[c * ec:(c + 1) * ec, :] = jnp.concatenate(rows, axis=0) * act.astype(BF16)
    acc_scr[...] += jnp.dot(vt_ref[...], pt_scr[...], preferred_element_type=F32)

    @pl.when(j == pl.num_programs(1) - 1)
    def _():
        z = ALPHA * x_ref[...] + mod_ref[0, 5:6, :] * acc_scr[...].T
        o_ref[...] = _layer_norm(z, g_ref[...], b_ref[...])


def _peer_dense(xt, u_tab, vt_tab, r2, e2, l, e1, x, mod, seq, ln_g, ln_b, tb=512, eb=1024, ec=256):
    n_tok = x.shape[0]
    assert mod.shape[0] == 1 or seq % tb == 0
    mod_b = (lambda i: (i * tb) // seq) if mod.shape[0] > 1 else (lambda i: 0)
    route_full =pl.BlockSpec((PEER_HEADS, N_KEYS, tb), lambda i, j: (0, 0, i))
    route_blk = pl.BlockSpec((PEER_HEADS, eb // N_KEYS, tb), lambda i, j: (0, j, i))
    vec = pl.BlockSpec((1, D_MODEL), lambda i, j: (0, 0))
    return pl.pallas_call(
        functools.partial(_peer_dense_body, ec=ec),
        grid=(n_tok // tb, N_EXPERTS // eb),
        in_specs=[
            pl.BlockSpec((D_MODEL, tb), lambda i, j: (0, i)),
            pl.BlockSpec((eb, D_MODEL), lambda i, j: (j, 0)),
            pl.BlockSpec((D_MODEL, eb), lambda i, j: (0, j)),
            route_full, route_full, route_blk, route_blk,
            pl.BlockSpec((tb, D_MODEL), lambda i, j: (i, 0)),
            pl.BlockSpec((1, 6, D_MODEL), lambda i, j: (mod_b(i), 0, 0)),
            vec, vec,
        ],
        out_specs=pl.BlockSpec((tb, D_MODEL), lambda i, j: (i, 0)),
        out_shape=jax.ShapeDtypeStruct((n_tok, D_MODEL), F32),
        scratch_shapes=[pltpu.VMEM((D_MODEL, tb), F32), pltpu.VMEM((eb, tb), BF16)],
        compiler_params=_cparams(("arbitrary", "arbitrary")),
        name="peer_dense",
    )(xt, u_tab, vt_tab, r2, e2, l, e1, x, mod, ln_g.reshape(1, D_MODEL), ln_b.reshape(1, D_MODEL))


def _peer_layer(x, mod, wq, k1, k2, u_tab, vt_tab, ln_g, ln_b):
    bsz, seq, _ = x.shape
    xf = x.reshape(bsz * seq, D_MODEL)
    xt, r2, e2, l, e1 = _peer_route(xf, mod, seq, wq, k1, k2)
    out = _peer_dense(xt, u_tab, vt_tab, r2, e2, l, e1, xf, mod, seq, ln_g, ln_b)
    return out.reshape(bsz, seq, D_MODEL)


def _trunk(x, mod, cache_k, cache_v, state, w):
    is_ctx = cache_k is None
    bsz = x.shape[0]
    q, k, v, p = _even_in(x, mod[0], w["even_w_in"], rope=not is_ctx)
    if is_ctx:
        attn = _attn_ctx(q, k, v, w["attn_sink"])
    else:
        attn = _attn_lat(q, k, v, cache_k, cache_v, w["attn_sink"])
    pooled = _pool(p, w["pool_w"], w["pool_scale"])
    x = _even_out(attn, pooled, x, mod[0], w["even_w_out"], w["ln1_g"][0], w["ln1_b"][0])
    x = _peer_layer(x, mod[0], w["peer_wq"][0], w["peer_k1"][0], w["peer_k2"][0], w["peer_u"][0], w["peer_v"][0],
                    w["ln2_g"][0], w["ln2_b"][0])
    h0 = jnp.zeros((bsz, 2, LRU_W), F32) if is_ctx else state
    x, fin = _odd_layer(x, mod[1], h0, w["odd_w_r"], w["odd_w_g"], w["conv_w"], w["conv_b"], w["gate_w"],
                        w["gate_a_b"], w["gate_x_b"], w["lru_lambda"], w["odd_w_out"], w["ln1_g"][1], w["ln1_b"][1])
    x = _peer_layer(x, mod[1], w["peer_wq"][1], w["peer_k1"][1], w["peer_k2"][1], w["peer_u"][1], w["peer_v"][1],
                    w["ln2_g"][1], w["ln2_b"][1])
    return x, k, v, fin


def kernel(x_prompt, x_sample, cache_attn_k, cache_attn_v, state_lru, c, c_ctx, ada_w, ada_b, ln1_g, ln1_b, ln2_g,
           ln2_b, even_w_in, attn_sink, pool_w, pool_scale, even_w_out, odd_w_in, conv_w, conv_b, gate_a_w, gate_a_b,
           gate_x_w, gate_x_b, lru_lambda, odd_w_out, peer_wq, peer_k1, peer_k2, peer_u, peer_v):
    batch, seq, _ = x_prompt.shape
    dec_batch = x_sample.shape[0]
    past = cache_attn_k.shape[2]
    w = dict(
        even_w_in=even_w_in[0].astype(BF16), even_w_out=even_w_out[0].astype(BF16), attn_sink=attn_sink[0],
        pool_w=pool_w[0].astype(BF16), pool_scale=pool_scale[0],
        odd_w_r=odd_w_in[0, :, :LRU_W].astype(BF16), odd_w_g=odd_w_in[0, :, LRU_W:].astype(BF16),
        conv_w=conv_w[0], conv_b=conv_b[0],
        gate_w=jnp.concatenate([gate_a_w[0], gate_x_w[0]], axis=-1).astype(BF16),
        gate_a_b=gate_a_b[0], gate_x_b=gate_x_b[0], lru_lambda=lru_lambda[0], odd_w_out=odd_w_out[0].astype(BF16),
        peer_wq=peer_wq.astype(BF16), peer_k1=peer_k1.astype(BF16), peer_k2=peer_k2.astype(BF16),
        peer_u=peer_u.astype(BF16), peer_v=jnp.swapaxes(peer_v.astype(BF16), 1, 2),
        ln1_g=ln1_g, ln1_b=ln1_b, ln2_g=ln2_g, ln2_b=ln2_b,
    )
    rows = SUBLANES * ((1 + dec_batch + SUBLANES - 1) // SUBLANES)
    cond = jnp.zeros((rows, D_MODEL), F32).at[0].set(c_ctx).at[1:1 + dec_batch].set(c)
    mod = _ada(cond, ada_w, ada_b)
    mod_ctx = mod[:, 0:1]
    mod_lat = mod[:, 1:1 + dec_batch]

    y_prompt, k_ctx, v_ctx, fin = _trunk(x_prompt, mod_ctx, None, None, None, w)
    ck = cache_attn_k[:, 0].reshape(dec_batch, past, KV_W)
    cv = cache_attn_v[:, 0].reshape(dec_batch, past, KV_W)
    y_sample, _, _, _ = _trunk(x_sample, mod_lat, ck, cv, state_lru[:, 0], w)

    new_k = k_ctx.reshape(batch, 1, seq, N_KV_HEADS, HEAD_DIM)
    new_v = v_ctx.reshape(batch, 1, seq, N_KV_HEADS, HEAD_DIM)
    new_state = fin.reshape(batch, 1, 2, LRU_W)
    return (y_prompt, y_sample, new_k, new_v, new_state)
```
